```python
import math
import jax, jax.numpy as jnp
from jax import lax
import numpy as np

D_MODEL = 1024
BATCH = 16
SEQ = 2048
DEPTH = 4

HEAD_DIM = 64
EPS = 1e-6
NEG_INF = -1e30
A_HEADS = 8
A_BRANCHES = ((128, 1), (512, 4), (2048, 16))
T5_BUCKETS = 32
T5_MAX_DIST = 1024
B_HEADS = 8
B_Q_RANK = 768
B_KV_RANK = 256
B_NOPE = 64
B_ROPE = 32
B_V = 64
ROPE_BASE = 10000.0
ATTN_BLOCK = 128
C_GROUPS = 8
C_GROUP_W = 64
C_CHUNK = 128
C_WIDTH = C_GROUPS * C_GROUP_W
D_HEADS = 8
GRID_W = 64
NA_ROWS = 8
NA_COLS = 16
FFN_HIDDEN = -(-8 * D_MODEL // (3 * 256)) * 256
A_WIDTH = A_HEADS * HEAD_DIM
D_WIDTH = D_HEADS * HEAD_DIM
EVEN_IN = 3 * A_WIDTH + B_Q_RANK + B_KV_RANK + B_ROPE
EVEN_MIX = A_WIDTH + B_HEADS * B_V
ODD_IN = 2 * C_WIDTH + 3 * D_WIDTH
ODD_MIX = C_WIDTH + D_WIDTH
N_EVEN = (DEPTH + 1) // 2
N_ODD = DEPTH // 2

kernel_name = 'hybrid_dilated_mla_gmlp_natten_encoder'


def _rmsnorm(x, g):
    xf = x.astype(jnp.float32)
    y = xf * lax.rsqrt(jnp.mean(xf * xf, -1, keepdims=True) + EPS) * g.astype(jnp.float32)
    return y.astype(x.dtype)


def _t5_bucket(rel):
    nb = T5_BUCKETS // 2
    max_exact = nb // 2
    n = np.abs(rel)
    large = max_exact + (np.log(np.maximum(n, 1) / max_exact) / math.log(T5_MAX_DIST / max_exact) * (nb - max_exact)).astype(np.int64)
    large = np.minimum(large, nb - 1)
    return ((rel > 0) * nb + np.where(n < max_exact, n, large)).astype(np.int32)


def _dilated_branch(q, k, v, t5_table, window, dil):
    B, S, H, hd = q.shape
    half = (window // 2) // dil
    L = S // dil
    nb = -(-L // half)
    Lp = nb * half

    def sub(t):
        return t.reshape(B, L, dil, H, hd).transpose(0, 2, 1, 3, 4)

    qs = jnp.pad(sub(q), ((0, 0), (0, 0), (0, Lp - L), (0, 0), (0, 0))).reshape(B, dil, nb, half, H, hd)

    def win(t):
        tp = jnp.pad(sub(t), ((0, 0), (0, 0), (half, Lp - L + half), (0, 0), (0, 0))).reshape(B, dil, nb + 2, half, H, hd)
        return jnp.concatenate([tp[:, :, 0:nb], tp[:, :, 1:nb + 1], tp[:, :, 2:nb + 2]], axis=3)

    kw, vw = win(k), win(v)
    qi = np.arange(half)[:, None]
    kj = np.arange(3 * half)[None, :]
    off = kj - half - qi
    key_idx = np.arange(nb)[:, None, None] * half - half + kj[None]
    valid = (np.abs(off) <= half)[None] & (key_idx >= 0) & (key_idx < L)
    bias = jnp.transpose(t5_table[_t5_bucket(off * dil)], (2, 0, 1)).astype(jnp.float32)
    s = jnp.einsum('bgnqhd,bgnkhd->bgnhqk', qs, kw).astype(jnp.float32) * (hd ** -0.5) + bias
    s = jnp.where(valid[None, None, :, None], s, NEG_INF)
    m = jnp.max(s, -1, keepdims=True)
    p = jnp.exp(s - m)
    den = jnp.sum(p, -1)
    o = jnp.einsum('bgnhqk,bgnkhd->bgnqhd', p, vw.astype(jnp.float32)) / jnp.transpose(den, (0, 1, 2, 4, 3))[..., None]
    lse = m[..., 0] + jnp.log(den)
    o = o.reshape(B, dil, Lp, H, hd)[:, :, :L].transpose(0, 2, 1, 3, 4).reshape(B, S, H, hd)
    lse = jnp.transpose(lse, (0, 1, 2, 4, 3)).reshape(B, dil, Lp, H)[:, :, :L].transpose(0, 2, 1, 3).reshape(B, S, H)
    return o, lse


def _dilated_mixture(q, k, v, t5_table):
    B, S, H, hd = q.shape
    outs, lses = zip(*[_dilated_branch(q, k, v, t5_table, w, d) for (w, d) in A_BRANCHES])
    wts = jax.nn.softmax(jnp.stack(lses, 0), axis=0)
    o = jnp.sum(wts[..., None] * jnp.stack(outs, 0), 0)
    return o.astype(q.dtype).reshape(B, S, H * hd)


def _rope_tables(S):
    pos = jnp.arange(S, dtype=jnp.float32)
    inv = 1.0 / (ROPE_BASE ** (jnp.arange(0, B_ROPE, 2, dtype=jnp.float32) / B_ROPE))
    ang = pos[:, None] * inv[None, :]
    return jnp.cos(ang), jnp.sin(ang)


def _apply_rope(x, cos, sin):
    hf = B_ROPE // 2
    shape = (1, x.shape[1]) + (1,) * (x.ndim - 3) + (hf,)
    c, s = cos.reshape(shape), sin.reshape(shape)
    xf = x.astype(jnp.float32)
    x1, x2 = xf[..., :hf], xf[..., hf:]
    return jnp.concatenate([x1 * c - x2 * s, x1 * s + x2 * c], -1).astype(x.dtype)


def _mla(c_q, c_kv, k_pe, q_gain, kv_gain, w_uq, w_ukv, cos, sin):
    B, S, _ = c_q.shape
    q = (_rmsnorm(c_q, q_gain) @ w_uq).reshape(B, S, B_HEADS, B_NOPE + B_ROPE)
    kv = (_rmsnorm(c_kv, kv_gain) @ w_ukv).reshape(B, S, B_HEADS, B_NOPE + B_V)
    q = jnp.concatenate([q[..., :B_NOPE], _apply_rope(q[..., B_NOPE:], cos, sin)], -1)
    k_pe = _apply_rope(k_pe, cos, sin)
    k = jnp.concatenate([kv[..., :B_NOPE], jnp.broadcast_to(k_pe[:, :, None, :], (B, S, B_HEADS, B_ROPE))], -1)
    v = kv[..., B_NOPE:]
    scale = (B_NOPE + B_ROPE) ** -0.5
    nq = S // ATTN_BLOCK
    qb = q.reshape(B, nq, ATTN_BLOCK, B_HEADS, B_NOPE + B_ROPE).transpose(1, 0, 2, 3, 4)

    def block(qblk):
        s = jnp.einsum('bqhd,bkhd->bhqk', qblk, k).astype(jnp.float32) * scale
        p = jax.nn.softmax(s, -1)
        return jnp.einsum('bhqk,bkhd->bqhd', p, v.astype(jnp.float32)).astype(v.dtype)

    o = lax.map(block, qb)
    return o.transpose(1, 0, 2, 3, 4).reshape(B, S, B_HEADS * B_V)


def _spatial_gating(z_in, v_gain, w_s, b_s):
    B, S, _ = z_in.shape
    z = jax.nn.gelu(z_in.astype(jnp.float32))
    u, v = z[..., :C_WIDTH], z[..., C_WIDTH:]
    mu = jnp.mean(v, -1, keepdims=True)
    var = jnp.mean((v - mu) ** 2, -1, keepdims=True)
    vn = (v - mu) * lax.rsqrt(var + EPS) * v_gain.astype(jnp.float32)
    vc = vn.reshape(B, S // C_CHUNK, C_CHUNK, C_GROUPS, C_GROUP_W)
    sv = jnp.einsum('gij,bnjgc->bnigc', w_s.astype(jnp.float32), vc) + jnp.transpose(b_s.astype(jnp.float32))[None, None, :, :, None]
    return (u * sv.reshape(B, S, C_WIDTH)).astype(z_in.dtype)


def _neighbourhood_attention(q, k, v, rpb):
    B, S, H, hd = q.shape
    rows = S // GRID_W
    kr = min(NA_ROWS, rows)
    n_cb = GRID_W // NA_COLS
    kcw = 2 * NA_COLS
    qcol = np.arange(GRID_W).reshape(n_cb, NA_COLS)
    qstart = np.clip(qcol - NA_COLS // 2, 0, GRID_W - NA_COLS)
    kb_start = np.clip(np.arange(n_cb) * NA_COLS - NA_COLS // 2, 0, GRID_W - kcw)
    kcol = kb_start[:, None] + np.arange(kcw)[None, :]
    col_mask = (kcol[:, None, :] >= qstart[..., None]) & (kcol[:, None, :] < qstart[..., None] + NA_COLS)
    dc = np.clip(kcol[:, None, :] - qcol[..., None] + NA_COLS - 1, 0, 2 * NA_COLS - 2)
    qg = q.reshape(B, rows, GRID_W, H, hd)
    kg = k.reshape(B, rows, GRID_W, H, hd)
    vg = v.reshape(B, rows, GRID_W, H, hd)
    scale = hd ** -0.5

    def row_fn(args):
        qr, i = args
        r0 = jnp.clip(i - kr // 2, 0, rows - kr)
        krows = lax.dynamic_slice_in_dim(kg, r0, kr, axis=1)
        vrows = lax.dynamic_slice_in_dim(vg, r0, kr, axis=1)
        kblk = jnp.stack([krows[:, :, int(s0):int(s0) + kcw] for s0 in kb_start], axis=2)
        vblk = jnp.stack([vrows[:, :, int(s0):int(s0) + kcw] for s0 in kb_start], axis=2)
        qb = qr.reshape(B, n_cb, NA_COLS, H, hd)
        s = jnp.einsum('bcqhd,bacmhd->bhcqam', qb, kblk).astype(jnp.float32) * scale
        dr = r0 + jnp.arange(kr) - i + NA_ROWS - 1
        bias = rpb[:, dr[:, None, None, None], dc[None]].astype(jnp.float32)
        s = s + jnp.transpose(bias, (0, 2, 3, 1, 4))[None]
        s = jnp.where(col_mask[:, :, None, :], s, NEG_INF)
        p = jax.nn.softmax(s.reshape(B, H, n_cb, NA_COLS, kr * kcw), -1).reshape(s.shape)
        o = jnp.einsum('bhcqam,bacmhd->bcqhd', p, vblk.astype(jnp.float32))
        return o.reshape(B, GRID_W, H, hd).astype(q.dtype)

    out = lax.map(row_fn, (qg.transpose(1, 0, 2, 3, 4), jnp.arange(rows, dtype=jnp.int32)))
    return out.transpose(1, 0, 2, 3, 4).reshape(B, S, H * hd)


def setup_inputs(seed: int = 0) -> dict:
    key = jax.random.key(seed)
    ks = jax.random.split(key, 20)

    def nrm(k, shape, scale):
        return jax.random.normal(k, shape, jnp.float32) * scale

    def gain(k, shape):
        return 1.0 + 0.05 * jax.random.normal(k, shape, jnp.float32)

    return {
        'x': nrm(ks[0], (BATCH, SEQ, D_MODEL), 1.0),
        't5_bias': nrm(ks[1], (T5_BUCKETS, A_HEADS), 0.2),
        'norm_mix': gain(ks[2], (DEPTH, D_MODEL)),
        'norm_ffn': gain(ks[3], (DEPTH, D_MODEL)),
        'ev_w_in': nrm(ks[4], (N_EVEN, D_MODEL, EVEN_IN), D_MODEL ** -0.5),
        'ev_q_gain': gain(ks[5], (N_EVEN, B_Q_RANK)),
        'ev_kv_gain': gain(ks[6], (N_EVEN, B_KV_RANK)),
        'ev_w_uq': nrm(ks[7], (N_EVEN, B_Q_RANK, B_HEADS * (B_NOPE + B_ROPE)), B_Q_RANK ** -0.5),
        'ev_w_ukv': nrm(ks[8], (N_EVEN, B_KV_RANK, B_HEADS * (B_NOPE + B_V)), B_KV_RANK ** -0.5),
        'ev_w_out': nrm(ks[9], (N_EVEN, EVEN_MIX, D_MODEL), EVEN_MIX ** -0.5),
        'od_w_in': nrm(ks[10], (N_ODD, D_MODEL, ODD_IN), D_MODEL ** -0.5),
        'od_v_gain': gain(ks[11], (N_ODD, C_WIDTH)),
        'od_w_s': nrm(ks[12], (N_ODD, C_GROUPS, C_CHUNK, C_CHUNK), C_CHUNK ** -0.5),
        'od_b_s': nrm(ks[13], (N_ODD, C_GROUPS, C_CHUNK), 0.1),
        'od_rpb': nrm(ks[14], (N_ODD, D_HEADS, 2 * NA_ROWS - 1, 2 * NA_COLS - 1), 0.2),
        'od_w_out': nrm(ks[15], (N_ODD, ODD_MIX, D_MODEL), ODD_MIX ** -0.5),
        'ffn_w_gu': nrm(ks[16], (DEPTH, D_MODEL, 2 * FFN_HIDDEN), D_MODEL ** -0.5),
        'ffn_w_down': nrm(ks[17], (DEPTH, FFN_HIDDEN, D_MODEL), FFN_HIDDEN ** -0.5),
        'final_gain': gain(ks[18], (D_MODEL,)),
    }


def reference(x, t5_bias, norm_mix, norm_ffn, ev_w_in, ev_q_gain, ev_kv_gain, ev_w_uq, ev_w_ukv, ev_w_out,
              od_w_in, od_v_gain, od_w_s, od_b_s, od_rpb, od_w_out, ffn_w_gu, ffn_w_down, final_gain):
    B, S, _ = x.shape
    cos, sin = _rope_tables(S)
    for layer in range(DEPTH):
        h = _rmsnorm(x, norm_mix[layer])
        j = layer // 2
        if layer % 2 == 0:
            p = h @ ev_w_in[j]
            qa, ka, va = [p[..., i * A_WIDTH:(i + 1) * A_WIDTH].reshape(B, S, A_HEADS, HEAD_DIM) for i in range(3)]
            o0 = 3 * A_WIDTH
            o1 = o0 + B_Q_RANK
            o2 = o1 + B_KV_RANK
            a_out = _dilated_mixture(qa, ka, va, t5_bias)
            b_out = _mla(p[..., o0:o1], p[..., o1:o2], p[..., o2:o2 + B_ROPE], ev_q_gain[j], ev_kv_gain[j],
                         ev_w_uq[j], ev_w_ukv[j], cos, sin)
            x = x + jnp.concatenate([a_out, b_out], -1) @ ev_w_out[j]
        else:
            p = h @ od_w_in[j]
            c_out = _spatial_gating(p[..., :2 * C_WIDTH], od_v_gain[j], od_w_s[j], od_b_s[j])
            base = 2 * C_WIDTH
            qd, kd, vd = [p[..., base + i * D_WIDTH:base + (i + 1) * D_WIDTH].reshape(B, S, D_HEADS, HEAD_DIM) for i in range(3)]
            d_out = _neighbourhood_attention(qd, kd, vd, od_rpb[j])
            x = x + jnp.concatenate([c_out, d_out], -1) @ od_w_out[j]
        h = _rmsnorm(x, norm_ffn[layer])
        gu = h @ ffn_w_gu[layer]
        x = x + (jax.nn.silu(gu[..., :FFN_HIDDEN]) * gu[..., FFN_HIDDEN:]) @ ffn_w_down[layer]
    return _rmsnorm(x, final_gain)
```

```python
import functools
import math

import jax
import jax.numpy as jnp
import numpy as np
from jax import lax
from jax.experimental import pallas as pl
from jax.experimental.pallas import tpu as pltpu

HEAD_DIM = 64
EPS = 1e-6
NEG_INF = -1e30
A_HEADS = 8
A_BRANCHES = ((128, 1), (512, 4), (2048, 16))
T5_BUCKETS = 32
T5_MAX_DIST = 1024
B_HEADS = 8
B_Q_RANK = 768
B_KV_RANK = 256
B_NOPE = 64
B_ROPE = 32
B_V = 64
ROPE_BASE = 10000.0
C_GROUPS = 8
C_GROUP_W = 64
C_CHUNK = 128
C_WIDTH = C_GROUPS * C_GROUP_W
D_HEADS = 8
GRID_W = 64
NA_ROWS = 8
NA_COLS = 16
A_WIDTH = A_HEADS * HEAD_DIM
D_WIDTH = D_HEADS * HEAD_DIM

LANES = 128
VMEM_LIMIT = 56 * 1024 * 1024
ROW_TILE = 512
MLA_Q_TILE = 256
DIL_Q = 128
DIL_HALF = 64
NA_QROWS = 4
NA_KROWS = 12

BF16 = jnp.bfloat16
F32 = jnp.float32


def _dot(a, b):
    return jnp.dot(a, b, preferred_element_type=F32)


def _dot_nt(a, b):
    return lax.dot_general(a, b, (((1,), (1,)), ((), ())), preferred_element_type=F32)


def _rms(x, g):
    return x * lax.rsqrt(jnp.mean(x * x, -1, keepdims=True) + EPS) * g


def _const_spec(shape):
    nd = len(shape)
    return pl.BlockSpec(shape, lambda *_: (0,) * nd, pipeline_mode=pl.Buffered(1))


def _params(sem):
    return pltpu.CompilerParams(dimension_semantics=sem, vmem_limit_bytes=VMEM_LIMIT)


def _rope(t, c, s1, s2):
    return t * c + pltpu.roll(t, LANES - B_ROPE // 2, 1) * s1 + pltpu.roll(t, B_ROPE // 2, 1) * s2


def _even_in_kernel(x_ref, g_ref, w_ref, qg_ref, kvg_ref, wuq_ref, wk_ref, wv_ref, c_ref, s1_ref, s2_ref,
                    qa_ref, ka_ref, va_ref, q_ref, k_ref, v_ref):
    h = _rms(x_ref[...], g_ref[...]).astype(BF16)
    aw = A_WIDTH
    qa_ref[...] = _dot(h, w_ref[:, 0:aw]).astype(BF16)
    ka_ref[...] = _dot(h, w_ref[:, aw:2 * aw]).astype(BF16)
    va_ref[...] = _dot(h, w_ref[:, 2 * aw:3 * aw]).astype(BF16)
    o0 = 3 * aw
    o1 = o0 + B_Q_RANK
    o2 = o1 + B_KV_RANK
    cq = _rms(_dot(h, w_ref[:, o0:o1]), qg_ref[...]).astype(BF16)
    ckv = _rms(_dot(h, w_ref[:, o1:o2]), kvg_ref[...]).astype(BF16)
    c, s1, s2 = c_ref[...], s1_ref[...], s2_ref[...]
    kpe = _rope(_dot(h, w_ref[:, o2:o2 + LANES]), c, s1, s2)
    q = _dot(cq, wuq_ref[...])
    kn = _dot(ckv, wk_ref[...])
    for hh in range(B_HEADS):
        sl = slice(LANES * hh, LANES * (hh + 1))
        q_ref[:, sl] = _rope(q[:, sl], c, s1, s2).astype(BF16)
        k_ref[:, sl] = (kn[:, sl] + kpe).astype(BF16)
    v_ref[...] = _dot(ckv, wv_ref[...]).astype(BF16)


def _even_in(x2, g, w_in, qg, kvg, wuq, wk, wv, rope_c, rope_s1, rope_s2, seq):
    t, d = x2.shape
    tm = ROW_TILE
    nseq = seq // tm
    row = lambda n: pl.BlockSpec((tm, n), lambda i: (i, 0))
    pos = pl.BlockSpec((tm, LANES), lambda i: (i % nseq, 0))
    qk_w = B_HEADS * LANES
    out_shapes = [jax.ShapeDtypeStruct((t, n), BF16) for n in (A_WIDTH, A_WIDTH, A_WIDTH, qk_w, qk_w, B_HEADS * B_V)]
    return pl.pallas_call(
        _even_in_kernel,
        grid=(t // tm,),
        in_specs=[row(d), _const_spec(g.shape), _const_spec(w_in.shape), _const_spec(qg.shape),
                  _const_spec(kvg.shape), _const_spec(wuq.shape), _const_spec(wk.shape), _const_spec(wv.shape),
                  pos, pos, pos],
        out_specs=[row(s.shape[1]) for s in out_shapes],
        out_shape=out_shapes,
        compiler_params=_params(("parallel",)),
        name="even_in",
    )(x2, g, w_in, qg, kvg, wuq, wk, wv, rope_c, rope_s1, rope_s2)


def _mla_kernel(q_ref, k_ref, v_ref, o_ref):
    scale = (B_NOPE + B_ROPE) ** -0.5
    v = v_ref[0]
    outs = []
    for h in range(2):
        sl = slice(LANES * h, LANES * (h + 1))
        s = _dot_nt(q_ref[0, :, sl], k_ref[0, :, sl]) * scale
        m = jnp.max(s, -1, keepdims=True)
        p = jnp.exp(s - m)
        den = jnp.sum(p, -1, keepdims=True)
        outs.append(_dot(p.astype(BF16), v) / den)
    lane = lax.broadcasted_iota(jnp.int32, outs[0].shape, 1)
    o_ref[0] = jnp.where(lane < B_V, outs[0], outs[1]).astype(BF16)


def _mla(q, k, v):
    b, s, _ = q.shape
    tq = MLA_Q_TILE
    return pl.pallas_call(
        _mla_kernel,
        grid=(b, B_HEADS // 2, s // tq),
        in_specs=[pl.BlockSpec((1, tq, 2 * LANES), lambda bi, hp, qi: (bi, qi, hp)),
                  pl.BlockSpec((1, s, 2 * LANES), lambda bi, hp, qi: (bi, 0, hp)),
                  pl.BlockSpec((1, s, LANES), lambda bi, hp, qi: (bi, 0, hp))],
        out_specs=pl.BlockSpec((1, tq, LANES), lambda bi, hp, qi: (bi, qi, hp)),
        out_shape=jax.ShapeDtypeStruct((b, s, B_HEADS * B_V), BF16),
        compiler_params=_params(("parallel", "parallel", "arbitrary")),
        name="mla_attn",
    )(q, k, v)


def _t5_bucket(rel):
    nb = T5_BUCKETS // 2
    max_exact = nb // 2
    n = np.abs(rel)
    large = max_exact + (np.log(np.maximum(n, 1) / max_exact) / math.log(T5_MAX_DIST / max_exact)
                         * (nb - max_exact)).astype(np.int64)
    large = np.minimum(large, nb - 1)
    return ((rel > 0) * nb + np.where(n < max_exact, n, large)).astype(np.int32)


def _dil_geometry(seq, dil):
    length = seq // dil
    assert length % DIL_Q == 0
    if length == DIL_Q:
        return length, DIL_Q, (0,)
    return length, 2 * DIL_Q, (0, DIL_HALF, DIL_Q)


def _dil_bias(t5_bias, seq, dil):
    _, kwid, shifts = _dil_geometry(seq, dil)
    qi = np.arange(DIL_Q)[None, :, None]
    kj = np.arange(kwid)[None, None, :]
    off = kj - qi - np.asarray(shifts)[:, None, None]
    valid = np.abs(off) <= DIL_HALF
    bias = jnp.transpose(t5_bias[_t5_bucket(off * dil)], (3, 0, 1, 2)).astype(F32)
    bias = jnp.where(valid[None], bias, NEG_INF)
    return bias.reshape(A_HEADS * len(shifts), DIL_Q, kwid)


def _dil_kernel(q_ref, k_ref, v_ref, b1_ref, b2_ref, b3_ref, o_ref, q32, k32, v32, ob0, ob1, ob2, lb0, lb1, lb2,
                *, seq):
    ob = (ob0, ob1, ob2)
    lb = (lb0, lb1, lb2)
    q32[...] = q_ref[0].astype(F32)
    k32[...] = k_ref[0].astype(F32)
    v32[...] = v_ref[0].astype(F32)
    scale = HEAD_DIM ** -0.5
    lane = lax.broadcasted_iota(jnp.int32, (DIL_Q, LANES), 1)
    head0 = lane < HEAD_DIM

    for bi, ((_, dil), b_ref) in enumerate(zip(A_BRANCHES, (b1_ref, b2_ref, b3_ref))):
        length, kwid, shifts = _dil_geometry(seq, dil)
        nqb = length // DIL_Q
        ncase = len(shifts)

        def rows(start, n, dil=dil):
            return pl.ds(start, n) if dil == 1 else pl.ds(start, n, stride=dil)

        def unit(u, carry, dil=dil, nqb=nqb, kwid=kwid, ncase=ncase, length=length, b_ref=b_ref, bi=bi, rows=rows):
            r = u // nqb
            qb = u % nqb
            qstart = r + dil * DIL_Q * qb
            kpos = jnp.clip(qb * DIL_Q - DIL_HALF, 0, length - kwid)
            kstart = r + dil * kpos
            if dil == 1:
                qstart = pl.multiple_of(qstart, DIL_Q)
                kstart = pl.multiple_of(kstart, DIL_HALF)
            case = 0 if ncase == 1 else jnp.where(qb == 0, 0, jnp.where(qb == nqb - 1, 2, 1))
            qf = q32[rows(qstart, DIL_Q), :]
            kf = k32[rows(kstart, kwid), :].astype(BF16)
            vf = v32[rows(kstart, kwid), :].astype(BF16)
            outs, lses = [], []
            for h in range(2):
                qh = jnp.where(head0 if h == 0 else jnp.logical_not(head0), qf, 0.0).astype(BF16)
                s = _dot_nt(qh, kf) * scale + b_ref[h * ncase + case]
                m = jnp.max(s, -1, keepdims=True)
                p = jnp.exp(s - m)
                den = jnp.sum(p, -1, keepdims=True)
                outs.append(_dot(p.astype(BF16), vf) / den)
                lses.append(m + jnp.log(den))
            ob[bi][rows(qstart, DIL_Q), :] = jnp.where(head0, outs[0], outs[1])
            lb[bi][rows(qstart, DIL_Q), :] = jnp.where(head0, lses[0], lses[1])
            return carry

        lax.fori_loop(0, seq // DIL_Q, unit, 0)

    def combine(c, carry):
        sl = pl.ds(pl.multiple_of(c * 256, 256), 256)
        l0, l1, l2 = lb0[sl, :], lb1[sl, :], lb2[sl, :]
        m = jnp.maximum(jnp.maximum(l0, l1), l2)
        w0, w1, w2 = jnp.exp(l0 - m), jnp.exp(l1 - m), jnp.exp(l2 - m)
        mix = (w0 * ob0[sl, :] + w1 * ob1[sl, :] + w2 * ob2[sl, :]) / (w0 + w1 + w2)
        o_ref[0, sl, :] = mix.astype(BF16)
        return carry

    lax.fori_loop(0, seq // 256, combine, 0)


def _dilated(qa, ka, va, biases):
    b, s, _ = qa.shape
    npair = A_HEADS // 2
    tok = pl.BlockSpec((1, s, LANES), lambda hp, bi: (bi, 0, hp))

    def bias_spec(arr):
        n = arr.shape[0] // npair
        return pl.BlockSpec((n,) + arr.shape[1:], lambda hp, bi: (hp, 0, 0))

    return pl.pallas_call(
        functools.partial(_dil_kernel, seq=s),
        grid=(npair, b),
        in_specs=[tok, tok, tok] + [bias_spec(a) for a in biases],
        out_specs=tok,
        out_shape=jax.ShapeDtypeStruct((b, s, A_WIDTH), BF16),
        scratch_shapes=[pltpu.VMEM((s, LANES), F32)] * 9,
        compiler_params=_params(("parallel", "arbitrary")),
        name="dilated_attn",
    )(qa, ka, va, *biases)


def _gelu(x):
    c = math.sqrt(2.0 / math.pi)
    return x * (0.5 * (1.0 + jnp.tanh(c * (x + 0.044715 * (x * x * x)))))


def _odd_in_kernel(x_ref, g_ref, w_ref, vg_ref, ws_ref, bs_ref, c_ref, qd_ref, kd_ref, vd_ref):
    h = _rms(x_ref[...], g_ref[...]).astype(BF16)
    base = 2 * C_WIDTH
    qd_ref[...] = _dot(h, w_ref[:, base:base + D_WIDTH]).astype(BF16)
    kd_ref[...] = _dot(h, w_ref[:, base + D_WIDTH:base + 2 * D_WIDTH]).astype(BF16)
    vd_ref[...] = _dot(h, w_ref[:, base + 2 * D_WIDTH:base + 3 * D_WIDTH]).astype(BF16)
    u = _gelu(_dot(h, w_ref[:, 0:C_WIDTH]))
    v = _gelu(_dot(h, w_ref[:, C_WIDTH:2 * C_WIDTH]))
    mu = jnp.mean(v, -1, keepdims=True)
    vc = v - mu
    var = jnp.mean(vc * vc, -1, keepdims=True)
    vn = (vc * lax.rsqrt(var + EPS) * vg_ref[...]).astype(BF16)
    lane = lax.broadcasted_iota(jnp.int32, (C_CHUNK, LANES), 1)
    group0 = lane < C_GROUP_W
    for c in range(x_ref.shape[0] // C_CHUNK):
        rs = slice(C_CHUNK * c, C_CHUNK * (c + 1))
        for pair in range(C_GROUPS // 2):
            ls = slice(LANES * pair, LANES * (pair + 1))
            vp = vn[rs, ls]
            sv = jnp.where(group0, _dot(ws_ref[2 * pair], vp), _dot(ws_ref[2 * pair + 1], vp)) + bs_ref[:, ls]
            c_ref[rs, ls] = (u[rs, ls] * sv).astype(BF16)


def _odd_in(x2, g, w_in, vg, ws, bs_tile):
    t, d = x2.shape
    tm = ROW_TILE
    row = lambda n: pl.BlockSpec((tm, n), lambda i: (i, 0))
    out_shapes = [jax.ShapeDtypeStruct((t, n), BF16) for n in (C_WIDTH, D_WIDTH, D_WIDTH, D_WIDTH)]
    return pl.pallas_call(
        _odd_in_kernel,
        grid=(t // tm,),
        in_specs=[row(d), _const_spec(g.shape), _const_spec(w_in.shape), _const_spec(vg.shape),
                  _const_spec(ws.shape), _const_spec(bs_tile.shape)],
        out_specs=[row(s.shape[1]) for s in out_shapes],
        out_shape=out_shapes,
        compiler_params=_params(("parallel",)),
        name="odd_in",
    )(x2, g, w_in, vg, ws, bs_tile)


def _na_geometry(rows):
    kr = min(NA_ROWS, rows)
    nunit = rows // NA_QROWS
    kstart = [int(np.clip(NA_QROWS * j - kr // 2, 0, rows - NA_KROWS)) for j in range(nunit)]
    keys = [(kstart[j] - NA_QROWS * j,
             tuple(int(np.clip(i - kr // 2, 0, rows - kr)) - i for i in range(NA_QROWS * j, NA_QROWS * (j + 1))))
            for j in range(nunit)]
    patterns = sorted(set(keys), key=keys.index)
    return kstart, [patterns.index(k) for k in keys], patterns


def _na_bias(rpb, rows):
    _, _, patterns = _na_geometry(rows)
    kr = min(NA_ROWS, rows)
    qa = np.repeat(np.arange(NA_QROWS), GRID_W)[:, None]
    qc = np.tile(np.arange(GRID_W), NA_QROWS)[:, None]
    ka = np.repeat(np.arange(NA_KROWS), GRID_W)[None, :]
    kc = np.tile(np.arange(GRID_W), NA_KROWS)[None, :]
    cstart = np.clip(qc - NA_COLS // 2, 0, GRID_W - NA_COLS)
    col_ok = (kc >= cstart) & (kc < cstart + NA_COLS)
    dc = np.clip(kc - qc + NA_COLS - 1, 0, 2 * NA_COLS - 2)
    tiles = []
    for kstart_rel, r0_rel in patterns:
        r0 = np.asarray(r0_rel)[qa]
        krel = kstart_rel + ka - qa
        row_ok = (krel >= r0) & (krel < r0 + kr)
        dr = np.clip(krel + NA_ROWS - 1, 0, 2 * NA_ROWS - 2)
        tiles.append(jnp.where((row_ok & col_ok)[None], rpb[:, dr, dc].astype(F32), NEG_INF))
    bias = jnp.stack(tiles, 1)
    return bias.reshape(D_HEADS * len(patterns), NA_QROWS * GRID_W, NA_KROWS * GRID_W)


def _na_kernel(q_ref, k_ref, v_ref, b_ref, o_ref, *, rows):
    kstart, pattern, patterns = _na_geometry(rows)
    npat = len(patterns)
    scale = HEAD_DIM ** -0.5
    nq = NA_QROWS * GRID_W
    nk = NA_KROWS * GRID_W
    lane = lax.broadcasted_iota(jnp.int32, (nq, LANES), 1)
    head0 = lane < HEAD_DIM
    for j in range(rows // NA_QROWS):
        q = q_ref[0, nq * j:nq * (j + 1), :].astype(F32)
        kw = k_ref[0, GRID_W * kstart[j]:GRID_W * kstart[j] + nk, :]
        vw = v_ref[0, GRID_W * kstart[j]:GRID_W * kstart[j] + nk, :]
        outs = []
        for h in range(2):
            qh = jnp.where(head0 if h == 0 else jnp.logical_not(head0), q, 0.0).astype(BF16)
            s = _dot_nt(qh, kw) * scale + b_ref[h * npat + pattern[j]]
            m = jnp.max(s, -1, keepdims=True)
            p = jnp.exp(s - m)
            den = jnp.sum(p, -1, keepdims=True)
            outs.append(_dot(p.astype(BF16), vw) / den)
        o_ref[0, nq * j:nq * (j + 1), :] = jnp.where(head0, outs[0], outs[1]).astype(BF16)


def _neighbourhood(qd, kd, vd, bias):
    b, s, _ = qd.shape
    npair = D_HEADS // 2
    tok = pl.BlockSpec((1, s, LANES), lambda hp, bi: (bi, 0, hp))
    nb = bias.shape[0] // npair
    return pl.pallas_call(
        functools.partial(_na_kernel, rows=s // GRID_W),
        grid=(npair, b),
        in_specs=[tok, tok, tok, pl.BlockSpec((nb,) + bias.shape[1:], lambda hp, bi: (hp, 0, 0))],
        out_specs=tok,
        out_shape=jax.ShapeDtypeStruct((b, s, D_WIDTH), BF16),
        compiler_params=_params(("parallel", "arbitrary")),
        name="neighbourhood_attn",
    )(qd, kd, vd, bias)


def _mix_ffn_kernel(x_ref, a_ref, b_ref, wo_ref, g_ref, wgu_ref, wd_ref, fg_ref, o_ref, *, final, nchunk):
    half = a_ref.shape[1]
    x = x_ref[...] + _dot(a_ref[...], wo_ref[0:half, :]) + _dot(b_ref[...], wo_ref[half:2 * half, :])
    h = _rms(x, g_ref[...]).astype(BF16)
    hidden = wd_ref.shape[0]
    hc = hidden // nchunk
    acc = x
    for c in range(nchunk):
        gate = _dot(h, wgu_ref[:, hc * c:hc * (c + 1)])
        up = _dot(h, wgu_ref[:, hidden + hc * c:hidden + hc * (c + 1)])
        act = (gate * (1.0 / (1.0 + jnp.exp(-gate))) * up).astype(BF16)
        acc = acc + _dot(act, wd_ref[hc * c:hc * (c + 1), :])
    o_ref[...] = _rms(acc, fg_ref[...]) if final else acc


def _mix_ffn(x2, a, b, wo, g, wgu, wd, fg, final):
    t, d = x2.shape
    tm = ROW_TILE
    row = lambda n: pl.BlockSpec((tm, n), lambda i: (i, 0))
    return pl.pallas_call(
        functools.partial(_mix_ffn_kernel, final=final, nchunk=1),
        grid=(t // tm,),
        in_specs=[row(d), row(a.shape[1]), row(b.shape[1]), _const_spec(wo.shape), _const_spec(g.shape),
                  _const_spec(wgu.shape), _const_spec(wd.shape), _const_spec(fg.shape)],
        out_specs=row(d),
        out_shape=jax.ShapeDtypeStruct((t, d), F32),
        compiler_params=_params(("parallel",)),
        name="mix_ffn",
    )(x2, a, b, wo, g, wgu, wd, fg)


def _rope_tables(seq):
    pos = jnp.arange(seq, dtype=F32)
    inv = 1.0 / (ROPE_BASE ** (jnp.arange(0, B_ROPE, 2, dtype=F32) / B_ROPE))
    ang = pos[:, None] * inv[None, :]
    cos, sin = jnp.cos(ang), jnp.sin(ang)
    hf = B_ROPE // 2
    ones = jnp.ones((seq, B_NOPE), F32)
    z = lambda n: jnp.zeros((seq, n), F32)
    tail = LANES - B_NOPE - B_ROPE
    c = jnp.concatenate([ones, cos, cos, z(tail)], 1)
    s1 = jnp.concatenate([z(B_NOPE), -sin, z(hf), z(tail)], 1)
    s2 = jnp.concatenate([z(B_NOPE), z(hf), sin, z(tail)], 1)
    return c, s1, s2


def _pad_heads(w, heads, width):
    k = w.shape[0]
    return jnp.pad(w.reshape(k, heads, width), ((0, 0), (0, 0), (0, LANES - width))).reshape(k, heads * LANES)


def kernel(x, t5_bias, norm_mix, norm_ffn, ev_w_in, ev_q_gain, ev_kv_gain, ev_w_uq, ev_w_ukv, ev_w_out,
           od_w_in, od_v_gain, od_w_s, od_b_s, od_rpb, od_w_out, ffn_w_gu, ffn_w_down, final_gain):
    bsz, seq, d = x.shape
    depth = norm_mix.shape[0]
    t = bsz * seq
    assert seq % ROW_TILE == 0 and seq % GRID_W == 0 and ROW_TILE % C_CHUNK == 0
    x2 = x.reshape(t, d)
    rope_c, rope_s1, rope_s2 = _rope_tables(seq)
    dil_bias = [_dil_bias(t5_bias, seq, dil) for _, dil in A_BRANCHES]
    row = lambda v: v.reshape(1, -1).astype(F32)
    tok3 = lambda a: a.reshape(bsz, seq, a.shape[-1])
    tok2 = lambda a: a.reshape(t, a.shape[-1])

    for layer in range(depth):
        j = layer // 2
        if layer % 2 == 0:
            w_in = ev_w_in[j]
            o2 = 3 * A_WIDTH + B_Q_RANK + B_KV_RANK
            w_in = jnp.concatenate([w_in[:, :o2], jnp.zeros((d, B_NOPE), F32), w_in[:, o2:],
                                    jnp.zeros((d, LANES - B_NOPE - B_ROPE), F32)], 1).astype(BF16)
            wuq = _pad_heads(ev_w_uq[j], B_HEADS, B_NOPE + B_ROPE).astype(BF16)
            wukv = ev_w_ukv[j].reshape(B_KV_RANK, B_HEADS, B_NOPE + B_V)
            wk = _pad_heads(wukv[:, :, :B_NOPE].reshape(B_KV_RANK, -1), B_HEADS, B_NOPE).astype(BF16)
            wv = wukv[:, :, B_NOPE:].reshape(B_KV_RANK, -1).astype(BF16)
            qa, ka, va, q, k, v = _even_in(x2, row(norm_mix[layer]), w_in, row(ev_q_gain[j]), row(ev_kv_gain[j]),
                                           wuq, wk, wv, rope_c, rope_s1, rope_s2, seq)
            m0 = tok2(_dilated(tok3(qa), tok3(ka), tok3(va), dil_bias))
            m1 = tok2(_mla(tok3(q), tok3(k), tok3(v)))
            wo = ev_w_out[j]
        else:
            bs_tile = jnp.repeat(jnp.transpose(od_b_s[j]), C_GROUP_W, axis=1).astype(F32)
            m0, qd, kd, vd = _odd_in(x2, row(norm_mix[layer]), od_w_in[j].astype(BF16), row(od_v_gain[j]),
                                     od_w_s[j].astype(BF16), bs_tile)
            m1 = tok2(_neighbourhood(tok3(qd), tok3(kd), tok3(vd), _na_bias(od_rpb[j], seq // GRID_W)))
            wo = od_w_out[j]
        x2 = _mix_ffn(x2, m0, m1, wo.astype(BF16), row(norm_ffn[layer]), ffn_w_gu[layer].astype(BF16),
                      ffn_w_down[layer].astype(BF16), row(final_gain), final=layer == depth - 1)
    return x2.reshape(bsz, seq, d)
```

```python
import functools
import math

import jax
import jax.numpy as jnp
import numpy as np
from jax import lax
from jax.experimental import pallas as pl
from jax.experimental.pallas import tpu as pltpu

HEAD_DIM = 64
EPS = 1e-6
NEG_INF = -1e30
A_HEADS = 8
A_BRANCHES = ((128, 1), (512, 4), (2048, 16))
T5_BUCKETS = 32
T5_MAX_DIST = 1024
B_HEADS = 8
B_Q_RANK = 768
B_KV_RANK = 256
B_NOPE = 64
B_ROPE = 32
B_V = 64
ROPE_BASE = 10000.0
C_GROUPS = 8
C_GROUP_W = 64
C_CHUNK = 128
C_WIDTH = C_GROUPS * C_GROUP_W
D_HEADS = 8
GRID_W = 64
NA_ROWS = 8
NA_COLS = 16
A_WIDTH = A_HEADS * HEAD_DIM
D_WIDTH = D_HEADS * HEAD_DIM

LANES = 128
VMEM_LIMIT = 56 * 1024 * 1024
ROW_TILE = 512
MLA_Q_TILE = 256
DIL_Q = 128
DIL_HALF = 64
NA_QROWS = 4
NA_KROWS = 12

BF16 = jnp.bfloat16
F32 = jnp.float32


def _dot(a, b):
    return jnp.dot(a, b, preferred_element_type=F32)


def _dot_nt(a, b):
    return lax.dot_general(a, b, (((1,), (1,)), ((), ())), preferred_element_type=F32)


def _rms(x, g):
    return x * lax.rsqrt(jnp.mean(x * x, -1, keepdims=True) + EPS) * g


def _const_spec(shape):
    nd = len(shape)
    return pl.BlockSpec(shape, lambda *_: (0,) * nd, pipeline_mode=pl.Buffered(1))


def _params(sem):
    return pltpu.CompilerParams(dimension_semantics=sem, vmem_limit_bytes=VMEM_LIMIT)


def _rope(t, c, s1, s2):
    return t * c + pltpu.roll(t, LANES - B_ROPE // 2, 1) * s1 + pltpu.roll(t, B_ROPE // 2, 1) * s2


def _even_in_kernel(x_ref, g_ref, w_ref, qg_ref, kvg_ref, wuq_ref, wk_ref, wv_ref, c_ref, s1_ref, s2_ref,
                    qa_ref, ka_ref, va_ref, q_ref, k_ref, v_ref):
    h = _rms(x_ref[...], g_ref[...]).astype(BF16)
    aw = A_WIDTH
    qa_ref[...] = _dot(h, w_ref[:, 0:aw]).astype(BF16)
    ka_ref[...] = _dot(h, w_ref[:, aw:2 * aw]).astype(BF16)
    va_ref[...] = _dot(h, w_ref[:, 2 * aw:3 * aw]).astype(BF16)
    o0 = 3 * aw
    o1 = o0 + B_Q_RANK
    o2 = o1 + B_KV_RANK
    cq = _rms(_dot(h, w_ref[:, o0:o1]), qg_ref[...]).astype(BF16)
    ckv = _rms(_dot(h, w_ref[:, o1:o2]), kvg_ref[...]).astype(BF16)
    c, s1, s2 = c_ref[...], s1_ref[...], s2_ref[...]
    kpe = _rope(_dot(h, w_ref[:, o2:o2 + LANES]), c, s1, s2)
    q = _dot(cq, wuq_ref[...])
    kn = _dot(ckv, wk_ref[...])
    for hh in range(B_HEADS):
        sl = slice(LANES * hh, LANES * (hh + 1))
        q_ref[:, sl] = _rope(q[:, sl], c, s1, s2).astype(BF16)
        k_ref[:, sl] = (kn[:, sl] + kpe).astype(BF16)
    v_ref[...] = _dot(ckv, wv_ref[...]).astype(BF16)


def _even_in(x2, g, w_in, qg, kvg, wuq, wk, wv, rope_c, rope_s1, rope_s2, seq):
    t, d = x2.shape
    tm = ROW_TILE
    nseq = seq // tm
    row = lambda n: pl.BlockSpec((tm, n), lambda i: (i, 0))
    pos = pl.BlockSpec((tm, LANES), lambda i: (i % nseq, 0))
    qk_w = B_HEADS * LANES
    out_shapes = [jax.ShapeDtypeStruct((t, n), BF16) for n in (A_WIDTH, A_WIDTH, A_WIDTH, qk_w, qk_w, B_HEADS * B_V)]
    return pl.pallas_call(
        _even_in_kernel,
        grid=(t // tm,),
        in_specs=[row(d), _const_spec(g.shape), _const_spec(w_in.shape), _const_spec(qg.shape),
                  _const_spec(kvg.shape), _const_spec(wuq.shape), _const_spec(wk.shape), _const_spec(wv.shape),
                  pos, pos, pos],
        out_specs=[row(s.shape[1]) for s in out_shapes],
        out_shape=out_shapes,
        compiler_params=_params(("parallel",)),
        name="even_in",
    )(x2, g, w_in, qg, kvg, wuq, wk, wv, rope_c, rope_s1, rope_s2)


def _mla_kernel(q_ref, k_ref, v_ref, o_ref):
    scale = (B_NOPE + B_ROPE) ** -0.5
    v = v_ref[0]
    outs = []
    for h in range(2):
        sl = slice(LANES * h, LANES * (h + 1))
        s = _dot_nt(q_ref[0, :, sl], k_ref[0, :, sl]) * scale
        m = jnp.max(s, -1, keepdims=True)
        p = jnp.exp(s - m)
        den = jnp.sum(p, -1, keepdims=True)
        outs.append(_dot(p.astype(BF16), v) / den)
    lane = lax.broadcasted_iota(jnp.int32, outs[0].shape, 1)
    o_ref[0] = jnp.where(lane < B_V, outs[0], outs[1]).astype(BF16)


def _mla(q, k, v):
    b, s, _ = q.shape
    tq = MLA_Q_TILE
    return pl.pallas_call(
        _mla_kernel,
        grid=(b, B_HEADS // 2, s // tq),
        in_specs=[pl.BlockSpec((1, tq, 2 * LANES), lambda bi, hp, qi: (bi, qi, hp)),
                  pl.BlockSpec((1, s, 2 * LANES), lambda bi, hp, qi: (bi, 0, hp)),
                  pl.BlockSpec((1, s, LANES), lambda bi, hp, qi: (bi, 0, hp))],
        out_specs=pl.BlockSpec((1, tq, LANES), lambda bi, hp, qi: (bi, qi, hp)),
        out_shape=jax.ShapeDtypeStruct((b, s, B_HEADS * B_V), BF16),
        compiler_params=_params(("parallel", "parallel", "arbitrary")),
        name="mla_attn",
    )(q, k, v)


def _t5_bucket(rel):
    nb = T5_BUCKETS // 2
    max_exact = nb // 2
    n = np.abs(rel)
    large = max_exact + (np.log(np.maximum(n, 1) / max_exact) / math.log(T5_MAX_DIST / max_exact)
                         * (nb - max_exact)).astype(np.int64)
    large = np.minimum(large, nb - 1)
    return ((rel > 0) * nb + np.where(n < max_exact, n, large)).astype(np.int32)


def _dil_geometry(seq, dil):
    length = seq // dil
    assert length % DIL_Q == 0
    if length == DIL_Q:
        return length, DIL_Q, (0,)
    return length, 2 * DIL_Q, (0, DIL_HALF, DIL_Q)


def _toeplitz(vec, nrow, ncol):
    p = vec.shape[-1]
    flat = jnp.tile(vec, (1,) * (vec.ndim - 1) + (nrow,))[..., :nrow * (p - 1)]
    return flat.reshape(vec.shape[:-1] + (nrow, p - 1))[..., :ncol]


def _dil_bias(t5_bias, seq, dil):
    _, kwid, shifts = _dil_geometry(seq, dil)
    period = 2 * (DIL_Q + kwid)
    k = np.arange(period)
    k = np.where(k < kwid, k, k - period)
    off = k[None, :] - np.asarray(shifts)[:, None]
    valid = np.abs(off) <= DIL_HALF
    vec = jnp.transpose(t5_bias[_t5_bucket(off * dil)], (2, 0, 1)).astype(F32)
    vec = jnp.where(valid[None], vec, NEG_INF)
    return _toeplitz(vec, DIL_Q, kwid).reshape(A_HEADS * len(shifts), DIL_Q, kwid)


def _dil_kernel(q_ref, k_ref, v_ref, b1_ref, b2_ref, b3_ref, o_ref, q32, k32, v32, ob0, ob1, ob2, lb0, lb1, lb2,
                *, seq):
    ob = (ob0, ob1, ob2)
    lb = (lb0, lb1, lb2)
    q32[...] = q_ref[0].astype(F32)
    k32[...] = k_ref[0].astype(F32)
    v32[...] = v_ref[0].astype(F32)
    scale = HEAD_DIM ** -0.5
    lane = lax.broadcasted_iota(jnp.int32, (DIL_Q, LANES), 1)
    head0 = lane < HEAD_DIM

    for bi, ((_, dil), b_ref) in enumerate(zip(A_BRANCHES, (b1_ref, b2_ref, b3_ref))):
        length, kwid, shifts = _dil_geometry(seq, dil)
        nqb = length // DIL_Q
        ncase = len(shifts)

        def rows(start, n, dil=dil):
            return pl.ds(start, n) if dil == 1 else pl.ds(start, n, stride=dil)

        def unit(u, carry, dil=dil, nqb=nqb, kwid=kwid, ncase=ncase, length=length, b_ref=b_ref, bi=bi, rows=rows):
            r = u // nqb
            qb = u % nqb
            qstart = r + dil * DIL_Q * qb
            kpos = jnp.clip(qb * DIL_Q - DIL_HALF, 0, length - kwid)
            kstart = r + dil * kpos
            if dil == 1:
                qstart = pl.multiple_of(qstart, DIL_Q)
                kstart = pl.multiple_of(kstart, DIL_HALF)
            case = 0 if ncase == 1 else jnp.where(qb == 0, 0, jnp.where(qb == nqb - 1, 2, 1))
            qf = q32[rows(qstart, DIL_Q), :]
            kf = k32[rows(kstart, kwid), :].astype(BF16)
            vf = v32[rows(kstart, kwid), :].astype(BF16)
            outs, lses = [], []
            for h in range(2):
                qh = jnp.where(head0 if h == 0 else jnp.logical_not(head0), qf, 0.0).astype(BF16)
                s = _dot_nt(qh, kf) * scale + b_ref[h * ncase + case]
                m = jnp.max(s, -1, keepdims=True)
                p = jnp.exp(s - m)
                den = jnp.sum(p, -1, keepdims=True)
                outs.append(_dot(p.astype(BF16), vf) / den)
                lses.append(m + jnp.log(den))
            ob[bi][rows(qstart, DIL_Q), :] = jnp.where(head0, outs[0], outs[1])
            lb[bi][rows(qstart, DIL_Q), :] = jnp.where(head0, lses[0], lses[1])
            return carry

        lax.fori_loop(0, seq // DIL_Q, unit, 0, unroll=4)

    def combine(c, carry):
        sl = pl.ds(pl.multiple_of(c * 256, 256), 256)
        l0, l1, l2 = lb0[sl, :], lb1[sl, :], lb2[sl, :]
        m = jnp.maximum(jnp.maximum(l0, l1), l2)
        w0, w1, w2 = jnp.exp(l0 - m), jnp.exp(l1 - m), jnp.exp(l2 - m)
        mix = (w0 * ob0[sl, :] + w1 * ob1[sl, :] + w2 * ob2[sl, :]) / (w0 + w1 + w2)
        o_ref[0, sl, :] = mix.astype(BF16)
        return carry

    lax.fori_loop(0, seq // 256, combine, 0)


def _dilated(qa, ka, va, biases):
    b, s, _ = qa.shape
    npair = A_HEADS // 2
    tok = pl.BlockSpec((1, s, LANES), lambda hp, bi: (bi, 0, hp))

    def bias_spec(arr):
        n = arr.shape[0] // npair
        return pl.BlockSpec((n,) + arr.shape[1:], lambda hp, bi: (hp, 0, 0))

    return pl.pallas_call(
        functools.partial(_dil_kernel, seq=s),
        grid=(npair, b),
        in_specs=[tok, tok, tok] + [bias_spec(a) for a in biases],
        out_specs=tok,
        out_shape=jax.ShapeDtypeStruct((b, s, A_WIDTH), BF16),
        scratch_shapes=[pltpu.VMEM((s, LANES), F32)] * 9,
        compiler_params=_params(("parallel", "arbitrary")),
        name="dilated_attn",
    )(qa, ka, va, *biases)


def _gelu(x):
    c = math.sqrt(2.0 / math.pi)
    return x * (0.5 * (1.0 + jnp.tanh(c * (x + 0.044715 * (x * x * x)))))


def _odd_in_kernel(x_ref, g_ref, w_ref, vg_ref, ws_ref, bs_ref, c_ref, qd_ref, kd_ref, vd_ref):
    h = _rms(x_ref[...], g_ref[...]).astype(BF16)
    base = 2 * C_WIDTH
    qd_ref[...] = _dot(h, w_ref[:, base:base + D_WIDTH]).astype(BF16)
    kd_ref[...] = _dot(h, w_ref[:, base + D_WIDTH:base + 2 * D_WIDTH]).astype(BF16)
    vd_ref[...] = _dot(h, w_ref[:, base + 2 * D_WIDTH:base + 3 * D_WIDTH]).astype(BF16)
    u = _gelu(_dot(h, w_ref[:, 0:C_WIDTH]))
    v = _gelu(_dot(h, w_ref[:, C_WIDTH:2 * C_WIDTH]))
    mu = jnp.mean(v, -1, keepdims=True)
    vc = v - mu
    var = jnp.mean(vc * vc, -1, keepdims=True)
    vn = (vc * lax.rsqrt(var + EPS) * vg_ref[...]).astype(BF16)
    lane = lax.broadcasted_iota(jnp.int32, (C_CHUNK, LANES), 1)
    group0 = lane < C_GROUP_W
    for c in range(x_ref.shape[0] // C_CHUNK):
        rs = slice(C_CHUNK * c, C_CHUNK * (c + 1))
        for pair in range(C_GROUPS // 2):
            ls = slice(LANES * pair, LANES * (pair + 1))
            vp = vn[rs, ls]
            sv = jnp.where(group0, _dot(ws_ref[2 * pair], vp), _dot(ws_ref[2 * pair + 1], vp)) + bs_ref[:, ls]
            c_ref[rs, ls] = (u[rs, ls] * sv).astype(BF16)


def _odd_in(x2, g, w_in, vg, ws, bs_tile):
    t, d = x2.shape
    tm = ROW_TILE
    row = lambda n: pl.BlockSpec((tm, n), lambda i: (i, 0))
    out_shapes = [jax.ShapeDtypeStruct((t, n), BF16) for n in (C_WIDTH, D_WIDTH, D_WIDTH, D_WIDTH)]
    return pl.pallas_call(
        _odd_in_kernel,
        grid=(t // tm,),
        in_specs=[row(d), _const_spec(g.shape), _const_spec(w_in.shape), _const_spec(vg.shape),
                  _const_spec(ws.shape), _const_spec(bs_tile.shape)],
        out_specs=[row(s.shape[1]) for s in out_shapes],
        out_shape=out_shapes,
        compiler_params=_params(("parallel",)),
        name="odd_in",
    )(x2, g, w_in, vg, ws, bs_tile)


def _na_geometry(rows):
    kr = min(NA_ROWS, rows)
    nunit = rows // NA_QROWS
    kstart = [int(np.clip(NA_QROWS * j - kr // 2, 0, rows - NA_KROWS)) for j in range(nunit)]
    keys = [(kstart[j] - NA_QROWS * j,
             tuple(int(np.clip(i - kr // 2, 0, rows - kr)) - i for i in range(NA_QROWS * j, NA_QROWS * (j + 1))))
            for j in range(nunit)]
    patterns = sorted(set(keys), key=keys.index)
    return kstart, [patterns.index(k) for k in keys], patterns


def _na_bias(rpb, rows):
    _, _, patterns = _na_geometry(rows)
    kr = min(NA_ROWS, rows)
    qa = np.repeat(np.arange(NA_QROWS), GRID_W)[:, None]
    qc = np.tile(np.arange(GRID_W), NA_QROWS)[:, None]
    ka = np.repeat(np.arange(NA_KROWS), GRID_W)[None, :]
    kc = np.tile(np.arange(GRID_W), NA_KROWS)[None, :]
    cstart = np.clip(qc - NA_COLS // 2, 0, GRID_W - NA_COLS)
    col_ok = (kc >= cstart) & (kc < cstart + NA_COLS)
    rpb = rpb.astype(F32)
    gap = jnp.zeros(rpb.shape[:-1] + (2 * GRID_W - (2 * NA_COLS - 1),), F32)
    by_col = _toeplitz(jnp.concatenate([rpb[..., NA_COLS - 1:], gap, rpb[..., :NA_COLS - 1]], -1), GRID_W, GRID_W)
    lo = min(p[0] for p in patterns) - (NA_QROWS - 1) + NA_ROWS - 1
    hi = max(p[0] for p in patterns) + NA_KROWS - 1 + NA_ROWS - 1
    pad_lo, pad_hi = max(0, -lo), max(0, hi - (2 * NA_ROWS - 2))
    by_col = jnp.pad(by_col, ((0, 0), (pad_lo, pad_hi), (0, 0), (0, 0)))
    tiles, masks = [], []
    for kstart_rel, r0_rel in patterns:
        per_qrow = []
        for a in range(NA_QROWS):
            first = kstart_rel - a + NA_ROWS - 1 + pad_lo
            per_qrow.append(by_col[:, first:first + NA_KROWS])
        tiles.append(jnp.stack(per_qrow, 1))
        r0 = np.asarray(r0_rel)[qa]
        krel = kstart_rel + ka - qa
        masks.append((krel >= r0) & (krel < r0 + kr) & col_ok)
    bias = jnp.transpose(jnp.stack(tiles, 1), (0, 1, 2, 4, 3, 5))
    bias = bias.reshape(D_HEADS, len(patterns), NA_QROWS * GRID_W, NA_KROWS * GRID_W)
    bias = jnp.where(np.stack(masks)[None], bias, NEG_INF)
    return bias.reshape(D_HEADS * len(patterns), NA_QROWS * GRID_W, NA_KROWS * GRID_W)


def _na_kernel(q_ref, k_ref, v_ref, b_ref, o_ref, *, rows):
    kstart, pattern, patterns = _na_geometry(rows)
    npat = len(patterns)
    scale = HEAD_DIM ** -0.5
    nq = NA_QROWS * GRID_W
    nk = NA_KROWS * GRID_W
    lane = lax.broadcasted_iota(jnp.int32, (nq, LANES), 1)
    head0 = lane < HEAD_DIM
    for j in range(rows // NA_QROWS):
        q = q_ref[0, nq * j:nq * (j + 1), :].astype(F32)
        kw = k_ref[0, GRID_W * kstart[j]:GRID_W * kstart[j] + nk, :]
        vw = v_ref[0, GRID_W * kstart[j]:GRID_W * kstart[j] + nk, :]
        outs = []
        for h in range(2):
            qh = jnp.where(head0 if h == 0 else jnp.logical_not(head0), q, 0.0).astype(BF16)
            s = _dot_nt(qh, kw) * scale + b_ref[h * npat + pattern[j]]
            m = jnp.max(s, -1, keepdims=True)
            p = jnp.exp(s - m)
            den = jnp.sum(p, -1, keepdims=True)
            outs.append(_dot(p.astype(BF16), vw) / den)
        o_ref[0, nq * j:nq * (j + 1), :] = jnp.where(head0, outs[0], outs[1]).astype(BF16)


def _neighbourhood(qd, kd, vd, bias):
    b, s, _ = qd.shape
    npair = D_HEADS // 2
    tok = pl.BlockSpec((1, s, LANES), lambda hp, bi: (bi, 0, hp))
    nb = bias.shape[0] // npair
    return pl.pallas_call(
        functools.partial(_na_kernel, rows=s // GRID_W),
        grid=(npair, b),
        in_specs=[tok, tok, tok, pl.BlockSpec((nb,) + bias.shape[1:], lambda hp, bi: (hp, 0, 0))],
        out_specs=tok,
        out_shape=jax.ShapeDtypeStruct((b, s, D_WIDTH), BF16),
        compiler_params=_params(("parallel", "arbitrary")),
        name="neighbourhood_attn",
    )(qd, kd, vd, bias)


def _mix_ffn_kernel(x_ref, a_ref, b_ref, wo_ref, g_ref, wgu_ref, wd_ref, fg_ref, o_ref, *, final, nchunk):
    half = a_ref.shape[1]
    x = x_ref[...] + _dot(a_ref[...], wo_ref[0:half, :]) + _dot(b_ref[...], wo_ref[half:2 * half, :])
    h = _rms(x, g_ref[...]).astype(BF16)
    hidden = wd_ref.shape[0]
    hc = hidden // nchunk
    acc = x
    for c in range(nchunk):
        gate = _dot(h, wgu_ref[:, hc * c:hc * (c + 1)])
        up = _dot(h, wgu_ref[:, hidden + hc * c:hidden + hc * (c + 1)])
        act = (gate * (1.0 / (1.0 + jnp.exp(-gate))) * up).astype(BF16)
        acc = acc + _dot(act, wd_ref[hc * c:hc * (c + 1), :])
    o_ref[...] = _rms(acc, fg_ref[...]) if final else acc


def _mix_ffn(x2, a, b, wo, g, wgu, wd, fg, final):
    t, d = x2.shape
    tm = ROW_TILE
    row = lambda n: pl.BlockSpec((tm, n), lambda i: (i, 0))
    return pl.pallas_call(
        functools.partial(_mix_ffn_kernel, final=final, nchunk=1),
        grid=(t // tm,),
        in_specs=[row(d), row(a.shape[1]), row(b.shape[1]), _const_spec(wo.shape), _const_spec(g.shape),
                  _const_spec(wgu.shape), _const_spec(wd.shape), _const_spec(fg.shape)],
        out_specs=row(d),
        out_shape=jax.ShapeDtypeStruct((t, d), F32),
        compiler_params=_params(("parallel",)),
        name="mix_ffn",
    )(x2, a, b, wo, g, wgu, wd, fg)


def _rope_tables(seq):
    pos = jnp.arange(seq, dtype=F32)
    inv = 1.0 / (ROPE_BASE ** (jnp.arange(0, B_ROPE, 2, dtype=F32) / B_ROPE))
    ang = pos[:, None] * inv[None, :]
    cos, sin = jnp.cos(ang), jnp.sin(ang)
    hf = B_ROPE // 2
    ones = jnp.ones((seq, B_NOPE), F32)
    z = lambda n: jnp.zeros((seq, n), F32)
    tail = LANES - B_NOPE - B_ROPE
    c = jnp.concatenate([ones, cos, cos, z(tail)], 1)
    s1 = jnp.concatenate([z(B_NOPE), -sin, z(hf), z(tail)], 1)
    s2 = jnp.concatenate([z(B_NOPE), z(hf), sin, z(tail)], 1)
    return c, s1, s2


def _pad_heads(w, heads, width):
    k = w.shape[0]
    return jnp.pad(w.reshape(k, heads, width), ((0, 0), (0, 0), (0, LANES - width))).reshape(k, heads * LANES)


def kernel(x, t5_bias, norm_mix, norm_ffn, ev_w_in, ev_q_gain, ev_kv_gain, ev_w_uq, ev_w_ukv, ev_w_out,
           od_w_in, od_v_gain, od_w_s, od_b_s, od_rpb, od_w_out, ffn_w_gu, ffn_w_down, final_gain):
    bsz, seq, d = x.shape
    depth = norm_mix.shape[0]
    t = bsz * seq
    assert seq % ROW_TILE == 0 and seq % GRID_W == 0 and ROW_TILE % C_CHUNK == 0
    x2 = x.reshape(t, d)
    rope_c, rope_s1, rope_s2 = _rope_tables(seq)
    dil_bias = [_dil_bias(t5_bias, seq, dil) for _, dil in A_BRANCHES]
    row = lambda v: v.reshape(1, -1).astype(F32)
    tok3 = lambda a: a.reshape(bsz, seq, a.shape[-1])
    tok2 = lambda a: a.reshape(t, a.shape[-1])

    for layer in range(depth):
        j = layer // 2
        if layer % 2 == 0:
            w_in = ev_w_in[j]
            o2 = 3 * A_WIDTH + B_Q_RANK + B_KV_RANK
            w_in = jnp.concatenate([w_in[:, :o2], jnp.zeros((d, B_NOPE), F32), w_in[:, o2:],
                                    jnp.zeros((d, LANES - B_NOPE - B_ROPE), F32)], 1).astype(BF16)
            wuq = _pad_heads(ev_w_uq[j], B_HEADS, B_NOPE + B_ROPE).astype(BF16)
            wukv = ev_w_ukv[j].reshape(B_KV_RANK, B_HEADS, B_NOPE + B_V)
            wk = _pad_heads(wukv[:, :, :B_NOPE].reshape(B_KV_RANK, -1), B_HEADS, B_NOPE).astype(BF16)
            wv = wukv[:, :, B_NOPE:].reshape(B_KV_RANK, -1).astype(BF16)
            qa, ka, va, q, k, v = _even_in(x2, row(norm_mix[layer]), w_in, row(ev_q_gain[j]), row(ev_kv_gain[j]),
                                           wuq, wk, wv, rope_c, rope_s1, rope_s2, seq)
            m0 = tok2(_dilated(tok3(qa), tok3(ka), tok3(va), dil_bias))
            m1 = tok2(_mla(tok3(q), tok3(k), tok3(v)))
            wo = ev_w_out[j]
        else:
            bs_tile = jnp.repeat(jnp.transpose(od_b_s[j]), C_GROUP_W, axis=1).astype(F32)
            m0, qd, kd, vd = _odd_in(x2, row(norm_mix[layer]), od_w_in[j].astype(BF16), row(od_v_gain[j]),
                                     od_w_s[j].astype(BF16), bs_tile)
            m1 = tok2(_neighbourhood(tok3(qd), tok3(kd), tok3(vd), _na_bias(od_rpb[j], seq // GRID_W)))
            wo = od_w_out[j]
        x2 = _mix_ffn(x2, m0, m1, wo.astype(BF16), row(norm_ffn[layer]), ffn_w_gu[layer].astype(BF16),
                      ffn_w_down[layer].astype(BF16), row(final_gain), final=layer == depth - 1)
    return x2.reshape(bsz, seq, d)
```

```python
import functools
import math

import jax
import jax.numpy as jnp
import numpy as np
from jax import lax
from jax.experimental import pallas as pl
from jax.experimental.pallas import tpu as pltpu

HEAD_DIM = 64
EPS = 1e-6
NEG_INF = -1e30
LOG2E = math.log2(math.e)
A_HEADS = 8
A_BRANCHES = ((128, 1), (512, 4), (2048, 16))
T5_BUCKETS = 32
T5_MAX_DIST = 1024
B_HEADS = 8
B_Q_RANK = 768
B_KV_RANK = 256
B_NOPE = 64
B_ROPE = 32
B_V = 64
ROPE_BASE = 10000.0
C_GROUPS = 8
C_GROUP_W = 64
C_CHUNK = 128
C_WIDTH = C_GROUPS * C_GROUP_W
D_HEADS = 8
GRID_W = 64
NA_ROWS = 8
NA_COLS = 16
A_WIDTH = A_HEADS * HEAD_DIM
D_WIDTH = D_HEADS * HEAD_DIM

LANES = 128
VMEM_LIMIT = 56 * 1024 * 1024
ROW_TILE = 512
MLA_Q_TILE = 512
DIL_Q = 128
DIL_HALF = 64
NA_QROWS = 4
NA_KROWS = 12

BF16 = jnp.bfloat16
F32 = jnp.float32


def _dot(a, b):
    return jnp.dot(a, b, preferred_element_type=F32)


def _dot_nt(a, b):
    return lax.dot_general(a, b, (((1,), (1,)), ((), ())), preferred_element_type=F32)


def _rms(x, g):
    return x * lax.rsqrt(jnp.mean(x * x, -1, keepdims=True) + EPS) * g


def _const_spec(shape):
    nd = len(shape)
    return pl.BlockSpec(shape, lambda *_: (0,) * nd, pipeline_mode=pl.Buffered(1))


def _params(sem):
    return pltpu.CompilerParams(dimension_semantics=sem, vmem_limit_bytes=VMEM_LIMIT)


def _rope(t, c, s1, s2):
    return t * c + pltpu.roll(t, LANES - B_ROPE // 2, 1) * s1 + pltpu.roll(t, B_ROPE // 2, 1) * s2


def _even_in_kernel(x_ref, g_ref, w_ref, qg_ref, kvg_ref, wuq_ref, wk_ref, wv_ref, c_ref, s1_ref, s2_ref,
                    qa_ref, ka_ref, va_ref, q_ref, k_ref, v_ref):
    h = _rms(x_ref[...], g_ref[...]).astype(BF16)
    aw = A_WIDTH
    qa_ref[...] = _dot(h, w_ref[:, 0:aw]).astype(BF16)
    ka_ref[...] = _dot(h, w_ref[:, aw:2 * aw]).astype(BF16)
    va_ref[...] = _dot(h, w_ref[:, 2 * aw:3 * aw]).astype(BF16)
    o0 = 3 * aw
    o1 = o0 + B_Q_RANK
    o2 = o1 + B_KV_RANK
    cq = _rms(_dot(h, w_ref[:, o0:o1]), qg_ref[...]).astype(BF16)
    ckv = _rms(_dot(h, w_ref[:, o1:o2]), kvg_ref[...]).astype(BF16)
    c, s1, s2 = c_ref[...], s1_ref[...], s2_ref[...]
    kpe = _rope(_dot(h, w_ref[:, o2:o2 + LANES]), c, s1, s2)
    q = _dot(cq, wuq_ref[...])
    kn = _dot(ckv, wk_ref[...])
    qscale = (B_NOPE + B_ROPE) ** -0.5 * LOG2E
    for hh in range(B_HEADS):
        sl = slice(LANES * hh, LANES * (hh + 1))
        q_ref[:, sl] = (_rope(q[:, sl], c, s1, s2) * qscale).astype(BF16)
        k_ref[:, sl] = (kn[:, sl] + kpe).astype(BF16)
    v_ref[...] = _dot(ckv, wv_ref[...]).astype(BF16)


def _even_in(x2, g, w_in, qg, kvg, wuq, wk, wv, rope_c, rope_s1, rope_s2, seq):
    t, d = x2.shape
    tm = ROW_TILE
    nseq = seq // tm
    row = lambda n: pl.BlockSpec((tm, n), lambda i: (i, 0))
    pos = pl.BlockSpec((tm, LANES), lambda i: (i % nseq, 0))
    qk_w = B_HEADS * LANES
    out_shapes = [jax.ShapeDtypeStruct((t, n), BF16) for n in (A_WIDTH, A_WIDTH, A_WIDTH, qk_w, qk_w, B_HEADS * B_V)]
    return pl.pallas_call(
        _even_in_kernel,
        grid=(t // tm,),
        in_specs=[row(d), _const_spec(g.shape), _const_spec(w_in.shape), _const_spec(qg.shape),
                  _const_spec(kvg.shape), _const_spec(wuq.shape), _const_spec(wk.shape), _const_spec(wv.shape),
                  pos, pos, pos],
        out_specs=[row(s.shape[1]) for s in out_shapes],
        out_shape=out_shapes,
        compiler_params=_params(("parallel",)),
        name="even_in",
    )(x2, g, w_in, qg, kvg, wuq, wk, wv, rope_c, rope_s1, rope_s2)


def _mla_kernel(q_ref, k_ref, v_ref, o_ref):
    v = v_ref[0]
    outs = []
    for h in range(2):
        sl = slice(LANES * h, LANES * (h + 1))
        s = _dot_nt(q_ref[0, :, sl], k_ref[0, :, sl])
        m = jnp.max(s, -1, keepdims=True)
        p = jnp.exp2(s - m)
        den = jnp.sum(p, -1, keepdims=True)
        outs.append(_dot(p.astype(BF16), v) / den)
    lane = lax.broadcasted_iota(jnp.int32, outs[0].shape, 1)
    o_ref[0] = jnp.where(lane < B_V, outs[0], outs[1]).astype(BF16)


def _mla(q, k, v):
    b, s, _ = q.shape
    tq = MLA_Q_TILE
    return pl.pallas_call(
        _mla_kernel,
        grid=(b, B_HEADS // 2, s // tq),
        in_specs=[pl.BlockSpec((1, tq, 2 * LANES), lambda bi, hp, qi: (bi, qi, hp)),
                  pl.BlockSpec((1, s, 2 * LANES), lambda bi, hp, qi: (bi, 0, hp)),
                  pl.BlockSpec((1, s, LANES), lambda bi, hp, qi: (bi, 0, hp))],
        out_specs=pl.BlockSpec((1, tq, LANES), lambda bi, hp, qi: (bi, qi, hp)),
        out_shape=jax.ShapeDtypeStruct((b, s, B_HEADS * B_V), BF16),
        compiler_params=_params(("parallel", "parallel", "arbitrary")),
        name="mla_attn",
    )(q, k, v)


def _t5_bucket(rel):
    nb = T5_BUCKETS // 2
    max_exact = nb // 2
    n = np.abs(rel)
    large = max_exact + (np.log(np.maximum(n, 1) / max_exact) / math.log(T5_MAX_DIST / max_exact)
                         * (nb - max_exact)).astype(np.int64)
    large = np.minimum(large, nb - 1)
    return ((rel > 0) * nb + np.where(n < max_exact, n, large)).astype(np.int32)


def _dil_geometry(seq, dil):
    length = seq // dil
    assert length % DIL_Q == 0
    if length == DIL_Q:
        return length, DIL_Q, (0,)
    return length, 2 * DIL_Q, (0, DIL_HALF, DIL_Q)


def _toeplitz(vec, nrow, ncol):
    p = vec.shape[-1]
    flat = jnp.tile(vec, (1,) * (vec.ndim - 1) + (nrow,))[..., :nrow * (p - 1)]
    return flat.reshape(vec.shape[:-1] + (nrow, p - 1))[..., :ncol]


def _dil_bias(t5_bias, seq, dil):
    _, kwid, shifts = _dil_geometry(seq, dil)
    period = 2 * (DIL_Q + kwid)
    k = np.arange(period)
    k = np.where(k < kwid, k, k - period)
    off = k[None, :] - np.asarray(shifts)[:, None]
    valid = np.abs(off) <= DIL_HALF
    vec = jnp.transpose(t5_bias[_t5_bucket(off * dil)], (2, 0, 1)).astype(F32)
    vec = jnp.where(valid[None], vec * LOG2E, NEG_INF)
    return _toeplitz(vec, DIL_Q, kwid).reshape(A_HEADS * len(shifts), DIL_Q, kwid)


def _dil_kernel(q_ref, k_ref, v_ref, b1_ref, b4_ref, b16_ref, o_ref,
                f0, f1, f2, f3, f4, f5,
                qa1, qb1, va1, vb1, qa4, qb4, k4, va4, vb4, qa16, qb16, k16, va16, vb16,
                ob1, ob4, ob16, lb1, lb4, lb16, s0, s1, p0, p1, inv, *, seq):
    assert tuple(d for _, d in A_BRANCHES) == (1, 4, 16)
    qscale = HEAD_DIM ** -0.5 * LOG2E
    len4, len16 = seq // 4, seq // 16

    def split_heads(t, scale=None):
        head0 = lax.broadcasted_iota(jnp.int32, t.shape, 1) < HEAD_DIM
        if scale is not None:
            t = t * scale
        return jnp.where(head0, t, 0.0).astype(BF16), jnp.where(head0, 0.0, t).astype(BF16)

    for c in range(4):
        rs = slice(len4 * c, len4 * (c + 1))
        t = q_ref[0, rs, :].astype(F32)
        f0[rs, :] = t
        qa1[rs, :], qb1[rs, :] = split_heads(t, qscale)
        f1[rs, :] = k_ref[0, rs, :].astype(F32)
        t = v_ref[0, rs, :].astype(F32)
        f2[rs, :] = t
        va1[rs, :], vb1[rs, :] = split_heads(t)
    for r in range(4):
        rs = slice(len4 * r, len4 * (r + 1))
        t = f0[pl.ds(r, len4, stride=4), :]
        f3[rs, :] = t
        qa4[rs, :], qb4[rs, :] = split_heads(t, qscale)
        t = f1[pl.ds(r, len4, stride=4), :]
        f4[rs, :] = t
        k4[rs, :] = t.astype(BF16)
        t = f2[pl.ds(r, len4, stride=4), :]
        f5[rs, :] = t
        va4[rs, :], vb4[rs, :] = split_heads(t)
    for r4 in range(4):
        for rp in range(4):
            rs = slice(len16 * (r4 + 4 * rp), len16 * (r4 + 4 * rp + 1))
            src = pl.ds(len4 * r4 + rp, len16, stride=4)
            qa16[rs, :], qb16[rs, :] = split_heads(f3[src, :], qscale)
            k16[rs, :] = f4[src, :].astype(BF16)
            va16[rs, :], vb16[rs, :] = split_heads(f5[src, :])

    def run_branch(dil, q_pair, kload, v_pair, b_ref, ob, lb):
        length, kwid, shifts = _dil_geometry(seq, dil)
        nqb = length // DIL_Q
        ncase = len(shifts)
        nunit = seq // DIL_Q

        def geom(u):
            seg, qb = divmod(u, nqb)
            kstart = seg * length + min(max(qb * DIL_Q - DIL_HALF, 0), length - kwid)
            case = 0 if ncase == 1 else (0 if qb == 0 else 2 if qb == nqb - 1 else 1)
            return slice(DIL_Q * u, DIL_Q * (u + 1)), slice(kstart, kstart + kwid), case

        for u in range(nunit):
            qs, ks, case = geom(u)
            kw = kload(ks)
            for h, s_ref in enumerate((s0, s1)):
                s_ref[qs, 0:kwid] = _dot_nt(q_pair[h][qs, :], kw) + b_ref[h * ncase + case]
        rows = 2 * DIL_Q
        head0 = lax.broadcasted_iota(jnp.int32, (rows, LANES), 1) < HEAD_DIM
        for c in range(seq // rows):
            rs = slice(rows * c, rows * (c + 1))
            stats = []
            for s_ref, p_ref in ((s0, p0), (s1, p1)):
                s = s_ref[rs, 0:kwid]
                m = jnp.max(s, -1, keepdims=True)
                p = jnp.exp2(s - m)
                p_ref[rs, 0:kwid] = p.astype(BF16)
                stats.append((m, jnp.sum(p, -1, keepdims=True)))
            den = jnp.where(head0, stats[0][1], stats[1][1])
            inv[rs, :] = 1.0 / den
            lb[rs, :] = jnp.where(head0, stats[0][0], stats[1][0]) + jnp.log2(den)
        for u in range(nunit):
            qs, ks, _ = geom(u)
            o = _dot(p0[qs, 0:kwid], v_pair[0][ks, :]) + _dot(p1[qs, 0:kwid], v_pair[1][ks, :])
            ob[qs, :] = o * inv[qs, :]

    run_branch(1, (qa1, qb1), lambda ks: k_ref[0, ks, :], (va1, vb1), b1_ref, ob1, lb1)
    run_branch(4, (qa4, qb4), lambda ks: k4[ks, :], (va4, vb4), b4_ref, ob4, lb4)
    run_branch(16, (qa16, qb16), lambda ks: k16[ks, :], (va16, vb16), b16_ref, ob16, lb16)

    for src16, tmp, dst in ((ob16, f2, f3), (lb16, f4, f5)):
        for r4 in range(4):
            for rp in range(4):
                tmp[pl.ds(len4 * r4 + rp, len16, stride=4), :] = src16[len16 * (r4 + 4 * rp):len16 * (r4 + 4 * rp + 1), :]
        for r in range(4):
            dst[pl.ds(r, len4, stride=4), :] = tmp[len4 * r:len4 * (r + 1), :]
    for src4, dst in ((ob4, f0), (lb4, f1)):
        for r in range(4):
            dst[pl.ds(r, len4, stride=4), :] = src4[len4 * r:len4 * (r + 1), :]

    def combine(c, carry):
        sl = pl.ds(pl.multiple_of(c * 256, 256), 256)
        l0, l1, l2 = lb1[sl, :], f1[sl, :], f5[sl, :]
        m = jnp.maximum(jnp.maximum(l0, l1), l2)
        w0, w1, w2 = jnp.exp2(l0 - m), jnp.exp2(l1 - m), jnp.exp2(l2 - m)
        mix = (w0 * ob1[sl, :] + w1 * f0[sl, :] + w2 * f3[sl, :]) / (w0 + w1 + w2)
        o_ref[0, sl, :] = mix.astype(BF16)
        return carry

    lax.fori_loop(0, seq // 256, combine, 0)


def _dilated(qa, ka, va, biases):
    b, s, _ = qa.shape
    npair = A_HEADS // 2
    tok = pl.BlockSpec((1, s, LANES), lambda hp, bi: (bi, 0, hp))

    def bias_spec(arr):
        n = arr.shape[0] // npair
        return pl.BlockSpec((n,) + arr.shape[1:], lambda hp, bi: (hp, 0, 0))

    return pl.pallas_call(
        functools.partial(_dil_kernel, seq=s),
        grid=(npair, b),
        in_specs=[tok, tok, tok] + [bias_spec(a) for a in biases],
        out_specs=tok,
        out_shape=jax.ShapeDtypeStruct((b, s, A_WIDTH), BF16),
        scratch_shapes=([pltpu.VMEM((s, LANES), F32)] * 6 + [pltpu.VMEM((s, LANES), BF16)] * 14
                        + [pltpu.VMEM((s, LANES), F32)] * 6 + [pltpu.VMEM((s, 2 * DIL_Q), F32)] * 2
                        + [pltpu.VMEM((s, 2 * DIL_Q), BF16)] * 2 + [pltpu.VMEM((s, LANES), F32)]),
        compiler_params=_params(("parallel", "arbitrary")),
        name="dilated_attn",
    )(qa, ka, va, *biases)


def _gelu(x):
    c = math.sqrt(2.0 / math.pi)
    return x * (0.5 * (1.0 + jnp.tanh(c * (x + 0.044715 * (x * x * x)))))


def _odd_in_kernel(x_ref, g_ref, w_ref, vg_ref, ws_ref, bs_ref, c_ref, qd_ref, kd_ref, vd_ref):
    h = _rms(x_ref[...], g_ref[...]).astype(BF16)
    base = 2 * C_WIDTH
    qd_ref[...] = (_dot(h, w_ref[:, base:base + D_WIDTH]) * (HEAD_DIM ** -0.5 * LOG2E)).astype(BF16)
    kd_ref[...] = _dot(h, w_ref[:, base + D_WIDTH:base + 2 * D_WIDTH]).astype(BF16)
    vd_ref[...] = _dot(h, w_ref[:, base + 2 * D_WIDTH:base + 3 * D_WIDTH]).astype(BF16)
    u = _gelu(_dot(h, w_ref[:, 0:C_WIDTH]))
    v = _gelu(_dot(h, w_ref[:, C_WIDTH:2 * C_WIDTH]))
    mu = jnp.mean(v, -1, keepdims=True)
    vc = v - mu
    var = jnp.mean(vc * vc, -1, keepdims=True)
    vn = (vc * lax.rsqrt(var + EPS) * vg_ref[...]).astype(BF16)
    lane = lax.broadcasted_iota(jnp.int32, (C_CHUNK, LANES), 1)
    group0 = lane < C_GROUP_W
    for c in range(x_ref.shape[0] // C_CHUNK):
        rs = slice(C_CHUNK * c, C_CHUNK * (c + 1))
        for pair in range(C_GROUPS // 2):
            ls = slice(LANES * pair, LANES * (pair + 1))
            vp = vn[rs, ls]
            sv = jnp.where(group0, _dot(ws_ref[2 * pair], vp), _dot(ws_ref[2 * pair + 1], vp)) + bs_ref[:, ls]
            c_ref[rs, ls] = (u[rs, ls] * sv).astype(BF16)


def _odd_in(x2, g, w_in, vg, ws, bs_tile):
    t, d = x2.shape
    tm = ROW_TILE
    row = lambda n: pl.BlockSpec((tm, n), lambda i: (i, 0))
    out_shapes = [jax.ShapeDtypeStruct((t, n), BF16) for n in (C_WIDTH, D_WIDTH, D_WIDTH, D_WIDTH)]
    return pl.pallas_call(
        _odd_in_kernel,
        grid=(t // tm,),
        in_specs=[row(d), _const_spec(g.shape), _const_spec(w_in.shape), _const_spec(vg.shape),
                  _const_spec(ws.shape), _const_spec(bs_tile.shape)],
        out_specs=[row(s.shape[1]) for s in out_shapes],
        out_shape=out_shapes,
        compiler_params=_params(("parallel",)),
        name="odd_in",
    )(x2, g, w_in, vg, ws, bs_tile)


def _na_geometry(rows):
    kr = min(NA_ROWS, rows)
    nunit = rows // NA_QROWS
    kstart = [int(np.clip(NA_QROWS * j - kr // 2, 0, rows - NA_KROWS)) for j in range(nunit)]
    keys = [(kstart[j] - NA_QROWS * j,
             tuple(int(np.clip(i - kr // 2, 0, rows - kr)) - i for i in range(NA_QROWS * j, NA_QROWS * (j + 1))))
            for j in range(nunit)]
    patterns = sorted(set(keys), key=keys.index)
    return kstart, [patterns.index(k) for k in keys], patterns


def _na_bias(rpb, rows):
    _, _, patterns = _na_geometry(rows)
    kr = min(NA_ROWS, rows)
    qa = np.repeat(np.arange(NA_QROWS), GRID_W)[:, None]
    qc = np.tile(np.arange(GRID_W), NA_QROWS)[:, None]
    ka = np.repeat(np.arange(NA_KROWS), GRID_W)[None, :]
    kc = np.tile(np.arange(GRID_W), NA_KROWS)[None, :]
    cstart = np.clip(qc - NA_COLS // 2, 0, GRID_W - NA_COLS)
    col_ok = (kc >= cstart) & (kc < cstart + NA_COLS)
    rpb = rpb.astype(F32)
    gap = jnp.zeros(rpb.shape[:-1] + (2 * GRID_W - (2 * NA_COLS - 1),), F32)
    by_col = _toeplitz(jnp.concatenate([rpb[..., NA_COLS - 1:], gap, rpb[..., :NA_COLS - 1]], -1), GRID_W, GRID_W)
    lo = min(p[0] for p in patterns) - (NA_QROWS - 1) + NA_ROWS - 1
    hi = max(p[0] for p in patterns) + NA_KROWS - 1 + NA_ROWS - 1
    pad_lo, pad_hi = max(0, -lo), max(0, hi - (2 * NA_ROWS - 2))
    by_col = jnp.pad(by_col, ((0, 0), (pad_lo, pad_hi), (0, 0), (0, 0)))
    tiles, masks = [], []
    for kstart_rel, r0_rel in patterns:
        per_qrow = []
        for a in range(NA_QROWS):
            first = kstart_rel - a + NA_ROWS - 1 + pad_lo
            per_qrow.append(by_col[:, first:first + NA_KROWS])
        tiles.append(jnp.stack(per_qrow, 1))
        r0 = np.asarray(r0_rel)[qa]
        krel = kstart_rel + ka - qa
        masks.append((krel >= r0) & (krel < r0 + kr) & col_ok)
    bias = jnp.transpose(jnp.stack(tiles, 1), (0, 1, 2, 4, 3, 5))
    bias = bias.reshape(D_HEADS, len(patterns), NA_QROWS * GRID_W, NA_KROWS * GRID_W)
    bias = jnp.where(np.stack(masks)[None], bias * LOG2E, NEG_INF)
    return bias.reshape(D_HEADS * len(patterns), NA_QROWS * GRID_W, NA_KROWS * GRID_W)


def _na_kernel(q_ref, k_ref, v_ref, b_ref, o_ref, qa, qb, va, vb, *, rows):
    kstart, pattern, patterns = _na_geometry(rows)
    npat = len(patterns)
    nq = NA_QROWS * GRID_W
    nk = NA_KROWS * GRID_W
    for c in range(rows // NA_QROWS):
        rs = slice(nq * c, nq * (c + 1))
        head0 = lax.broadcasted_iota(jnp.int32, (nq, LANES), 1) < HEAD_DIM
        t = q_ref[0, rs, :].astype(F32)
        qa[rs, :] = jnp.where(head0, t, 0.0).astype(BF16)
        qb[rs, :] = jnp.where(head0, 0.0, t).astype(BF16)
        t = v_ref[0, rs, :].astype(F32)
        va[rs, :] = jnp.where(head0, t, 0.0).astype(BF16)
        vb[rs, :] = jnp.where(head0, 0.0, t).astype(BF16)
    for j in range(rows // NA_QROWS):
        qs = slice(nq * j, nq * (j + 1))
        ks = slice(GRID_W * kstart[j], GRID_W * kstart[j] + nk)
        kw = k_ref[0, ks, :]
        out = None
        for h, (q_h, v_h) in enumerate(((qa, va), (qb, vb))):
            s = _dot_nt(q_h[qs, :], kw) + b_ref[h * npat + pattern[j]]
            m = jnp.max(s, -1, keepdims=True)
            p = jnp.exp2(s - m)
            den = jnp.sum(p, -1, keepdims=True)
            o = _dot(p.astype(BF16), v_h[ks, :]) * (1.0 / den)
            out = o if out is None else out + o
        o_ref[0, qs, :] = out.astype(BF16)


def _neighbourhood(qd, kd, vd, bias):
    b, s, _ = qd.shape
    npair = D_HEADS // 2
    tok = pl.BlockSpec((1, s, LANES), lambda hp, bi: (bi, 0, hp))
    nb = bias.shape[0] // npair
    return pl.pallas_call(
        functools.partial(_na_kernel, rows=s // GRID_W),
        grid=(npair, b),
        in_specs=[tok, tok, tok, pl.BlockSpec((nb,) + bias.shape[1:], lambda hp, bi: (hp, 0, 0))],
        out_specs=tok,
        out_shape=jax.ShapeDtypeStruct((b, s, D_WIDTH), BF16),
        scratch_shapes=[pltpu.VMEM((s, LANES), BF16)] * 4,
        compiler_params=_params(("parallel", "arbitrary")),
        name="neighbourhood_attn",
    )(qd, kd, vd, bias)


def _mix_ffn_kernel(x_ref, a_ref, b_ref, wo_ref, g_ref, wgu_ref, wd_ref, fg_ref, o_ref, *, final, nchunk):
    half = a_ref.shape[1]
    x = x_ref[...] + _dot(a_ref[...], wo_ref[0:half, :]) + _dot(b_ref[...], wo_ref[half:2 * half, :])
    h = _rms(x, g_ref[...]).astype(BF16)
    hidden = wd_ref.shape[0]
    hc = hidden // nchunk
    acc = x
    for c in range(nchunk):
        gate = _dot(h, wgu_ref[:, hc * c:hc * (c + 1)])
        up = _dot(h, wgu_ref[:, hidden + hc * c:hidden + hc * (c + 1)])
        act = (gate * (1.0 / (1.0 + jnp.exp(-gate))) * up).astype(BF16)
        acc = acc + _dot(act, wd_ref[hc * c:hc * (c + 1), :])
    o_ref[...] = _rms(acc, fg_ref[...]) if final else acc


def _mix_ffn(x2, a, b, wo, g, wgu, wd, fg, final):
    t, d = x2.shape
    tm = ROW_TILE
    row = lambda n: pl.BlockSpec((tm, n), lambda i: (i, 0))
    return pl.pallas_call(
        functools.partial(_mix_ffn_kernel, final=final, nchunk=1),
        grid=(t // tm,),
        in_specs=[row(d), row(a.shape[1]), row(b.shape[1]), _const_spec(wo.shape), _const_spec(g.shape),
                  _const_spec(wgu.shape), _const_spec(wd.shape), _const_spec(fg.shape)],
        out_specs=row(d),
        out_shape=jax.ShapeDtypeStruct((t, d), F32),
        compiler_params=_params(("parallel",)),
        name="mix_ffn",
    )(x2, a, b, wo, g, wgu, wd, fg)


def _rope_tables(seq):
    pos = jnp.arange(seq, dtype=F32)
    inv = 1.0 / (ROPE_BASE ** (jnp.arange(0, B_ROPE, 2, dtype=F32) / B_ROPE))
    ang = pos[:, None] * inv[None, :]
    cos, sin = jnp.cos(ang), jnp.sin(ang)
    hf = B_ROPE // 2
    ones = jnp.ones((seq, B_NOPE), F32)
    z = lambda n: jnp.zeros((seq, n), F32)
    tail = LANES - B_NOPE - B_ROPE
    c = jnp.concatenate([ones, cos, cos, z(tail)], 1)
    s1 = jnp.concatenate([z(B_NOPE), -sin, z(hf), z(tail)], 1)
    s2 = jnp.concatenate([z(B_NOPE), z(hf), sin, z(tail)], 1)
    return c, s1, s2


def _pad_heads(w, heads, width):
    k = w.shape[0]
    return jnp.pad(w.reshape(k, heads, width), ((0, 0), (0, 0), (0, LANES - width))).reshape(k, heads * LANES)


def kernel(x, t5_bias, norm_mix, norm_ffn, ev_w_in, ev_q_gain, ev_kv_gain, ev_w_uq, ev_w_ukv, ev_w_out,
           od_w_in, od_v_gain, od_w_s, od_b_s, od_rpb, od_w_out, ffn_w_gu, ffn_w_down, final_gain):
    bsz, seq, d = x.shape
    depth = norm_mix.shape[0]
    t = bsz * seq
    assert seq % ROW_TILE == 0 and seq % GRID_W == 0 and ROW_TILE % C_CHUNK == 0
    x2 = x.reshape(t, d)
    rope_c, rope_s1, rope_s2 = _rope_tables(seq)
    dil_bias = [_dil_bias(t5_bias, seq, dil) for _, dil in A_BRANCHES]
    row = lambda v: v.reshape(1, -1).astype(F32)
    tok3 = lambda a: a.reshape(bsz, seq, a.shape[-1])
    tok2 = lambda a: a.reshape(t, a.shape[-1])

    for layer in range(depth):
        j = layer // 2
        if layer % 2 == 0:
            w_in = ev_w_in[j]
            o2 = 3 * A_WIDTH + B_Q_RANK + B_KV_RANK
            w_in = jnp.concatenate([w_in[:, :o2], jnp.zeros((d, B_NOPE), F32), w_in[:, o2:],
                                    jnp.zeros((d, LANES - B_NOPE - B_ROPE), F32)], 1).astype(BF16)
            wuq = _pad_heads(ev_w_uq[j], B_HEADS, B_NOPE + B_ROPE).astype(BF16)
            wukv = ev_w_ukv[j].reshape(B_KV_RANK, B_HEADS, B_NOPE + B_V)
            wk = _pad_heads(wukv[:, :, :B_NOPE].reshape(B_KV_RANK, -1), B_HEADS, B_NOPE).astype(BF16)
            wv = wukv[:, :, B_NOPE:].reshape(B_KV_RANK, -1).astype(BF16)
            qa, ka, va, q, k, v = _even_in(x2, row(norm_mix[layer]), w_in, row(ev_q_gain[j]), row(ev_kv_gain[j]),
                                           wuq, wk, wv, rope_c, rope_s1, rope_s2, seq)
            m0 = tok2(_dilated(tok3(qa), tok3(ka), tok3(va), dil_bias))
            m1 = tok2(_mla(tok3(q), tok3(k), tok3(v)))
            wo = ev_w_out[j]
        else:
            bs_tile = jnp.repeat(jnp.transpose(od_b_s[j]), C_GROUP_W, axis=1).astype(F32)
            m0, qd, kd, vd = _odd_in(x2, row(norm_mix[layer]), od_w_in[j].astype(BF16), row(od_v_gain[j]),
                                     od_w_s[j].astype(BF16), bs_tile)
            m1 = tok2(_neighbourhood(tok3(qd), tok3(kd), tok3(vd), _na_bias(od_rpb[j], seq // GRID_W)))
            wo = od_w_out[j]
        x2 = _mix_ffn(x2, m0, m1, wo.astype(BF16), row(norm_ffn[layer]), ffn_w_gu[layer].astype(BF16),
                      ffn_w_down[layer].astype(BF16), row(final_gain), final=layer == depth - 1)
    return x2.reshape(bsz, seq, d)
```

```python
import functools
import math

import jax
import jax.numpy as jnp
import numpy as np
from jax import lax
from jax.experimental import pallas as pl
from jax.experimental.pallas import tpu as pltpu

HEAD_DIM = 64
EPS = 1e-6
NEG_INF = -1e30
LOG2E = math.log2(math.e)
A_HEADS = 8
A_BRANCHES = ((128, 1), (512, 4), (2048, 16))
T5_BUCKETS = 32
T5_MAX_DIST = 1024
B_HEADS = 8
B_Q_RANK = 768
B_KV_RANK = 256
B_NOPE = 64
B_ROPE = 32
B_V = 64
ROPE_BASE = 10000.0
C_GROUPS = 8
C_GROUP_W = 64
C_CHUNK = 128
C_WIDTH = C_GROUPS * C_GROUP_W
D_HEADS = 8
GRID_W = 64
NA_ROWS = 8
NA_COLS = 16
A_WIDTH = A_HEADS * HEAD_DIM
D_WIDTH = D_HEADS * HEAD_DIM

LANES = 128
VMEM_LIMIT = 56 * 1024 * 1024
ROW_TILE = 512
MLA_Q_TILE = 512
DIL_Q = 128
DIL_HALF = 64
DIL_GROUP = 4
NA_QROWS = 4
NA_KROWS = 12

BF16 = jnp.bfloat16
F32 = jnp.float32


def _dot(a, b):
    return jnp.dot(a, b, preferred_element_type=F32)


def _dot_nt(a, b):
    return lax.dot_general(a, b, (((1,), (1,)), ((), ())), preferred_element_type=F32)


def _rms(x, g):
    return x * lax.rsqrt(jnp.mean(x * x, -1, keepdims=True) + EPS) * g


def _const_spec(shape):
    nd = len(shape)
    return pl.BlockSpec(shape, lambda *_: (0,) * nd, pipeline_mode=pl.Buffered(1))


def _params(sem):
    return pltpu.CompilerParams(dimension_semantics=sem, vmem_limit_bytes=VMEM_LIMIT)


def _pipeline3(njobs, stage_a, stage_b, stage_c):
    for step in range(njobs + 2):
        for lag, stage in enumerate((stage_a, stage_b, stage_c)):
            j = step - lag
            if 0 <= j < njobs:
                stage(j, j % 2)


def _rope(t, c, s1, s2):
    return t * c + pltpu.roll(t, LANES - B_ROPE // 2, 1) * s1 + pltpu.roll(t, B_ROPE // 2, 1) * s2


def _even_in_kernel(x_ref, g_ref, w_ref, qg_ref, kvg_ref, wuq_ref, wk_ref, wv_ref, c_ref, s1_ref, s2_ref,
                    qa_ref, ka_ref, va_ref, q_ref, k_ref, v_ref):
    h = _rms(x_ref[...], g_ref[...]).astype(BF16)
    aw = A_WIDTH
    qa_ref[...] = _dot(h, w_ref[:, 0:aw]).astype(BF16)
    ka_ref[...] = _dot(h, w_ref[:, aw:2 * aw]).astype(BF16)
    va_ref[...] = _dot(h, w_ref[:, 2 * aw:3 * aw]).astype(BF16)
    o0 = 3 * aw
    o1 = o0 + B_Q_RANK
    o2 = o1 + B_KV_RANK
    cq = _rms(_dot(h, w_ref[:, o0:o1]), qg_ref[...]).astype(BF16)
    ckv = _rms(_dot(h, w_ref[:, o1:o2]), kvg_ref[...]).astype(BF16)
    c, s1, s2 = c_ref[...], s1_ref[...], s2_ref[...]
    kpe = _rope(_dot(h, w_ref[:, o2:o2 + LANES]), c, s1, s2)
    q = _dot(cq, wuq_ref[...])
    kn = _dot(ckv, wk_ref[...])
    qscale = (B_NOPE + B_ROPE) ** -0.5 * LOG2E
    for hh in range(B_HEADS):
        sl = slice(LANES * hh, LANES * (hh + 1))
        q_ref[:, sl] = (_rope(q[:, sl], c, s1, s2) * qscale).astype(BF16)
        k_ref[:, sl] = (kn[:, sl] + kpe).astype(BF16)
    v_ref[...] = _dot(ckv, wv_ref[...]).astype(BF16)


def _even_in(x2, g, w_in, qg, kvg, wuq, wk, wv, rope_c, rope_s1, rope_s2, seq):
    t, d = x2.shape
    tm = ROW_TILE
    nseq = seq // tm
    row = lambda n: pl.BlockSpec((tm, n), lambda i: (i, 0))
    pos = pl.BlockSpec((tm, LANES), lambda i: (i % nseq, 0))
    qk_w = B_HEADS * LANES
    out_shapes = [jax.ShapeDtypeStruct((t, n), BF16) for n in (A_WIDTH, A_WIDTH, A_WIDTH, qk_w, qk_w, B_HEADS * B_V)]
    return pl.pallas_call(
        _even_in_kernel,
        grid=(t // tm,),
        in_specs=[row(d), _const_spec(g.shape), _const_spec(w_in.shape), _const_spec(qg.shape),
                  _const_spec(kvg.shape), _const_spec(wuq.shape), _const_spec(wk.shape), _const_spec(wv.shape),
                  pos, pos, pos],
        out_specs=[row(s.shape[1]) for s in out_shapes],
        out_shape=out_shapes,
        compiler_params=_params(("parallel",)),
        name="even_in",
    )(x2, g, w_in, qg, kvg, wuq, wk, wv, rope_c, rope_s1, rope_s2)


def _mla_kernel(q_ref, k_ref, v_ref, o_ref, s_a, s_b, p_a, p_b, inv_a, inv_b, va, vb, acc):
    seq = q_ref.shape[1]
    sub = MLA_Q_TILE
    head0 = lax.broadcasted_iota(jnp.int32, (sub, LANES), 1) < B_V
    for t in range(seq // sub):
        rs = slice(sub * t, sub * (t + 1))
        x = v_ref[0, rs, :].astype(F32)
        va[rs, :] = jnp.where(head0, x, 0.0).astype(BF16)
        vb[rs, :] = jnp.where(head0, 0.0, x).astype(BF16)
    jobs = [(t, h) for t in range(seq // sub) for h in range(2)]
    s_buf, p_buf, inv_buf = (s_a, s_b), (p_a, p_b), (inv_a, inv_b)

    def scores(j, slot):
        t, h = jobs[j]
        sl = slice(LANES * h, LANES * (h + 1))
        s_buf[slot][...] = _dot_nt(q_ref[0, sub * t:sub * (t + 1), sl], k_ref[0, :, sl])

    def softmax(j, slot):
        s = s_buf[slot][...]
        m = jnp.max(s, -1, keepdims=True)
        p = jnp.exp2(s - m)
        p_buf[slot][...] = p.astype(BF16)
        inv_buf[slot][...] = jnp.broadcast_to(1.0 / jnp.sum(p, -1, keepdims=True), (sub, LANES))

    def values(j, slot):
        t, h = jobs[j]
        o = _dot(p_buf[slot][...], (va, vb)[h][...]) * inv_buf[slot][...]
        if h == 0:
            acc[...] = o
        else:
            o_ref[0, sub * t:sub * (t + 1), :] = (acc[...] + o).astype(BF16)

    _pipeline3(len(jobs), scores, softmax, values)


def _mla(q, k, v):
    b, s, _ = q.shape
    sub = MLA_Q_TILE
    return pl.pallas_call(
        _mla_kernel,
        grid=(b, B_HEADS // 2),
        in_specs=[pl.BlockSpec((1, s, 2 * LANES), lambda bi, hp: (bi, 0, hp)),
                  pl.BlockSpec((1, s, 2 * LANES), lambda bi, hp: (bi, 0, hp)),
                  pl.BlockSpec((1, s, LANES), lambda bi, hp: (bi, 0, hp))],
        out_specs=pl.BlockSpec((1, s, LANES), lambda bi, hp: (bi, 0, hp)),
        out_shape=jax.ShapeDtypeStruct((b, s, B_HEADS * B_V), BF16),
        scratch_shapes=([pltpu.VMEM((sub, s), F32)] * 2 + [pltpu.VMEM((sub, s), BF16)] * 2
                        + [pltpu.VMEM((sub, LANES), F32)] * 2 + [pltpu.VMEM((s, LANES), BF16)] * 2
                        + [pltpu.VMEM((sub, LANES), F32)]),
        compiler_params=_params(("parallel", "parallel")),
        name="mla_attn",
    )(q, k, v)


def _t5_bucket(rel):
    nb = T5_BUCKETS // 2
    max_exact = nb // 2
    n = np.abs(rel)
    large = max_exact + (np.log(np.maximum(n, 1) / max_exact) / math.log(T5_MAX_DIST / max_exact)
                         * (nb - max_exact)).astype(np.int64)
    large = np.minimum(large, nb - 1)
    return ((rel > 0) * nb + np.where(n < max_exact, n, large)).astype(np.int32)


def _dil_geometry(seq, dil):
    length = seq // dil
    assert length % DIL_Q == 0
    if length == DIL_Q:
        return length, DIL_Q, (0,)
    return length, 2 * DIL_Q, (0, DIL_HALF, DIL_Q)


def _toeplitz(vec, nrow, ncol):
    p = vec.shape[-1]
    flat = jnp.tile(vec, (1,) * (vec.ndim - 1) + (nrow,))[..., :nrow * (p - 1)]
    return flat.reshape(vec.shape[:-1] + (nrow, p - 1))[..., :ncol]


def _dil_bias(t5_bias, seq, dil):
    _, kwid, shifts = _dil_geometry(seq, dil)
    period = 2 * (DIL_Q + kwid)
    k = np.arange(period)
    k = np.where(k < kwid, k, k - period)
    off = k[None, :] - np.asarray(shifts)[:, None]
    valid = np.abs(off) <= DIL_HALF
    vec = jnp.transpose(t5_bias[_t5_bucket(off * dil)], (2, 0, 1)).astype(F32)
    vec = jnp.where(valid[None], vec * LOG2E, NEG_INF)
    return _toeplitz(vec, DIL_Q, kwid).reshape(A_HEADS * len(shifts), DIL_Q, kwid)


def _dil_kernel(q_ref, k_ref, v_ref, b1_ref, b4_ref, b16_ref, o_ref,
                f0, f1, f2, f3, f4, f5,
                qa1, qb1, va1, vb1, qa4, qb4, k4, va4, vb4, qa16, qb16, k16, va16, vb16,
                ob1, ob4, ob16, lb1, lb4, lb16, s0, s1, s2, s3, p0, p1, p2, p3, i0, i1, *, seq):
    s_buf, p_buf, inv_buf = (s0, s1, s2, s3), (p0, p1, p2, p3), (i0, i1)
    assert tuple(d for _, d in A_BRANCHES) == (1, 4, 16)
    qscale = HEAD_DIM ** -0.5 * LOG2E
    len4, len16 = seq // 4, seq // 16

    def split_heads(t, scale=None):
        head0 = lax.broadcasted_iota(jnp.int32, t.shape, 1) < HEAD_DIM
        if scale is not None:
            t = t * scale
        return jnp.where(head0, t, 0.0).astype(BF16), jnp.where(head0, 0.0, t).astype(BF16)

    for c in range(4):
        rs = slice(len4 * c, len4 * (c + 1))
        t = q_ref[0, rs, :].astype(F32)
        f0[rs, :] = t
        qa1[rs, :], qb1[rs, :] = split_heads(t, qscale)
        f1[rs, :] = k_ref[0, rs, :].astype(F32)
        t = v_ref[0, rs, :].astype(F32)
        f2[rs, :] = t
        va1[rs, :], vb1[rs, :] = split_heads(t)
    for r in range(4):
        rs = slice(len4 * r, len4 * (r + 1))
        t = f0[pl.ds(r, len4, stride=4), :]
        f3[rs, :] = t
        qa4[rs, :], qb4[rs, :] = split_heads(t, qscale)
        t = f1[pl.ds(r, len4, stride=4), :]
        f4[rs, :] = t
        k4[rs, :] = t.astype(BF16)
        t = f2[pl.ds(r, len4, stride=4), :]
        f5[rs, :] = t
        va4[rs, :], vb4[rs, :] = split_heads(t)
    for r4 in range(4):
        for rp in range(4):
            rs = slice(len16 * (r4 + 4 * rp), len16 * (r4 + 4 * rp + 1))
            src = pl.ds(len4 * r4 + rp, len16, stride=4)
            qa16[rs, :], qb16[rs, :] = split_heads(f3[src, :], qscale)
            k16[rs, :] = f4[src, :].astype(BF16)
            va16[rs, :], vb16[rs, :] = split_heads(f5[src, :])

    branches = ((1, (qa1, qb1), lambda ks: k_ref[0, ks, :], (va1, vb1), b1_ref, ob1, lb1),
                (4, (qa4, qb4), lambda ks: k4[ks, :], (va4, vb4), b4_ref, ob4, lb4),
                (16, (qa16, qb16), lambda ks: k16[ks, :], (va16, vb16), b16_ref, ob16, lb16))
    ngroup = seq // (DIL_Q * DIL_GROUP)
    grows = DIL_Q * DIL_GROUP
    head0 = lax.broadcasted_iota(jnp.int32, (grows, LANES), 1) < HEAD_DIM

    def geom(job):
        dil = branches[job // ngroup][0]
        length, kwid, shifts = _dil_geometry(seq, dil)
        nqb = length // DIL_Q
        units = []
        for i in range(DIL_GROUP):
            u = (job % ngroup) * DIL_GROUP + i
            seg, qb = divmod(u, nqb)
            kstart = seg * length + min(max(qb * DIL_Q - DIL_HALF, 0), length - kwid)
            case = 0 if len(shifts) == 1 else (0 if qb == 0 else 2 if qb == nqb - 1 else 1)
            if dil == 1:
                out = slice(DIL_Q * u, DIL_Q * (u + 1))
            elif dil == 4:
                out = pl.ds(seg + dil * DIL_Q * qb, DIL_Q, stride=4)
            else:
                out = pl.ds(len4 * (u % 4) + u // 4, DIL_Q, stride=4)
            units.append((slice(DIL_Q * u, DIL_Q * (u + 1)), slice(DIL_Q * i, DIL_Q * (i + 1)),
                          slice(kstart, kstart + kwid), case, out))
        return kwid, len(shifts), units

    def scores(job, slot):
        _, q_pair, kload, _, b_ref, _, _ = branches[job // ngroup]
        kwid, ncase, units = geom(job)
        for qs, js, ks, case, _ in units:
            kw = kload(ks)
            for h in range(2):
                s_buf[2 * slot + h][js, 0:kwid] = _dot_nt(q_pair[h][qs, :], kw) + b_ref[h * ncase + case]

    def softmax(job, slot):
        lb = branches[job // ngroup][6]
        kwid, _, units = geom(job)
        stats = []
        for h in range(2):
            s = s_buf[2 * slot + h][:, 0:kwid]
            m = jnp.max(s, -1, keepdims=True)
            p = jnp.exp2(s - m)
            p_buf[2 * slot + h][:, 0:kwid] = p.astype(BF16)
            stats.append((m, jnp.sum(p, -1, keepdims=True)))
        den = jnp.where(head0, stats[0][1], stats[1][1])
        inv_buf[slot][...] = 1.0 / den
        lse = jnp.where(head0, stats[0][0], stats[1][0]) + jnp.log2(den)
        for _, js, _, _, out in units:
            lb[out, :] = lse[js, :]

    def values(job, slot):
        _, _, _, v_pair, _, ob, _ = branches[job // ngroup]
        kwid, _, units = geom(job)
        for _, js, ks, _, out in units:
            o = (_dot(p_buf[2 * slot][js, 0:kwid], v_pair[0][ks, :])
                 + _dot(p_buf[2 * slot + 1][js, 0:kwid], v_pair[1][ks, :]))
            ob[out, :] = o * inv_buf[slot][js, :]

    _pipeline3(len(branches) * ngroup, scores, softmax, values)

    for src16, dst in ((ob16, f0), (lb16, f1)):
        for r in range(4):
            dst[pl.ds(r, len4, stride=4), :] = src16[len4 * r:len4 * (r + 1), :]

    def combine(c, carry):
        sl = pl.ds(pl.multiple_of(c * 256, 256), 256)
        l0, l1, l2 = lb1[sl, :], lb4[sl, :], f1[sl, :]
        m = jnp.maximum(jnp.maximum(l0, l1), l2)
        w0, w1, w2 = jnp.exp2(l0 - m), jnp.exp2(l1 - m), jnp.exp2(l2 - m)
        mix = (w0 * ob1[sl, :] + w1 * ob4[sl, :] + w2 * f0[sl, :]) / (w0 + w1 + w2)
        o_ref[0, sl, :] = mix.astype(BF16)
        return carry

    lax.fori_loop(0, seq // 256, combine, 0)


def _dilated(qa, ka, va, biases):
    b, s, _ = qa.shape
    npair = A_HEADS // 2
    tok = pl.BlockSpec((1, s, LANES), lambda hp, bi: (bi, 0, hp))

    def bias_spec(arr):
        n = arr.shape[0] // npair
        return pl.BlockSpec((n,) + arr.shape[1:], lambda hp, bi: (hp, 0, 0))

    return pl.pallas_call(
        functools.partial(_dil_kernel, seq=s),
        grid=(npair, b),
        in_specs=[tok, tok, tok] + [bias_spec(a) for a in biases],
        out_specs=tok,
        out_shape=jax.ShapeDtypeStruct((b, s, A_WIDTH), BF16),
        scratch_shapes=([pltpu.VMEM((s, LANES), F32)] * 6 + [pltpu.VMEM((s, LANES), BF16)] * 14
                        + [pltpu.VMEM((s, LANES), F32)] * 6
                        + [pltpu.VMEM((DIL_Q * DIL_GROUP, 2 * DIL_Q), F32)] * 4
                        + [pltpu.VMEM((DIL_Q * DIL_GROUP, 2 * DIL_Q), BF16)] * 4
                        + [pltpu.VMEM((DIL_Q * DIL_GROUP, LANES), F32)] * 2),
        compiler_params=_params(("parallel", "arbitrary")),
        name="dilated_attn",
    )(qa, ka, va, *biases)


def _gelu(x):
    c = math.sqrt(2.0 / math.pi)
    return x * (0.5 * (1.0 + jnp.tanh(c * (x + 0.044715 * (x * x * x)))))


def _odd_in_kernel(x_ref, g_ref, w_ref, vg_ref, ws_ref, bs_ref, c_ref, qd_ref, kd_ref, vd_ref):
    h = _rms(x_ref[...], g_ref[...]).astype(BF16)
    base = 2 * C_WIDTH
    qd_ref[...] = (_dot(h, w_ref[:, base:base + D_WIDTH]) * (HEAD_DIM ** -0.5 * LOG2E)).astype(BF16)
    kd_ref[...] = _dot(h, w_ref[:, base + D_WIDTH:base + 2 * D_WIDTH]).astype(BF16)
    vd_ref[...] = _dot(h, w_ref[:, base + 2 * D_WIDTH:base + 3 * D_WIDTH]).astype(BF16)
    u = _gelu(_dot(h, w_ref[:, 0:C_WIDTH]))
    v = _gelu(_dot(h, w_ref[:, C_WIDTH:2 * C_WIDTH]))
    mu = jnp.mean(v, -1, keepdims=True)
    vc = v - mu
    var = jnp.mean(vc * vc, -1, keepdims=True)
    vn = (vc * lax.rsqrt(var + EPS) * vg_ref[...]).astype(BF16)
    lane = lax.broadcasted_iota(jnp.int32, (C_CHUNK, LANES), 1)
    group0 = lane < C_GROUP_W
    for c in range(x_ref.shape[0] // C_CHUNK):
        rs = slice(C_CHUNK * c, C_CHUNK * (c + 1))
        for pair in range(C_GROUPS // 2):
            ls = slice(LANES * pair, LANES * (pair + 1))
            vp = vn[rs, ls]
            sv = jnp.where(group0, _dot(ws_ref[2 * pair], vp), _dot(ws_ref[2 * pair + 1], vp)) + bs_ref[:, ls]
            c_ref[rs, ls] = (u[rs, ls] * sv).astype(BF16)


def _odd_in(x2, g, w_in, vg, ws, bs_tile):
    t, d = x2.shape
    tm = ROW_TILE
    row = lambda n: pl.BlockSpec((tm, n), lambda i: (i, 0))
    out_shapes = [jax.ShapeDtypeStruct((t, n), BF16) for n in (C_WIDTH, D_WIDTH, D_WIDTH, D_WIDTH)]
    return pl.pallas_call(
        _odd_in_kernel,
        grid=(t // tm,),
        in_specs=[row(d), _const_spec(g.shape), _const_spec(w_in.shape), _const_spec(vg.shape),
                  _const_spec(ws.shape), _const_spec(bs_tile.shape)],
        out_specs=[row(s.shape[1]) for s in out_shapes],
        out_shape=out_shapes,
        compiler_params=_params(("parallel",)),
        name="odd_in",
    )(x2, g, w_in, vg, ws, bs_tile)


def _na_geometry(rows):
    kr = min(NA_ROWS, rows)
    nunit = rows // NA_QROWS
    kstart = [int(np.clip(NA_QROWS * j - kr // 2, 0, rows - NA_KROWS)) for j in range(nunit)]
    keys = [(kstart[j] - NA_QROWS * j,
             tuple(int(np.clip(i - kr // 2, 0, rows - kr)) - i for i in range(NA_QROWS * j, NA_QROWS * (j + 1))))
            for j in range(nunit)]
    patterns = sorted(set(keys), key=keys.index)
    return kstart, [patterns.index(k) for k in keys], patterns


def _na_bias(rpb, rows):
    _, _, patterns = _na_geometry(rows)
    kr = min(NA_ROWS, rows)
    qa = np.repeat(np.arange(NA_QROWS), GRID_W)[:, None]
    qc = np.tile(np.arange(GRID_W), NA_QROWS)[:, None]
    ka = np.repeat(np.arange(NA_KROWS), GRID_W)[None, :]
    kc = np.tile(np.arange(GRID_W), NA_KROWS)[None, :]
    cstart = np.clip(qc - NA_COLS // 2, 0, GRID_W - NA_COLS)
    col_ok = (kc >= cstart) & (kc < cstart + NA_COLS)
    rpb = rpb.astype(F32)
    gap = jnp.zeros(rpb.shape[:-1] + (2 * GRID_W - (2 * NA_COLS - 1),), F32)
    by_col = _toeplitz(jnp.concatenate([rpb[..., NA_COLS - 1:], gap, rpb[..., :NA_COLS - 1]], -1), GRID_W, GRID_W)
    lo = min(p[0] for p in patterns) - (NA_QROWS - 1) + NA_ROWS - 1
    hi = max(p[0] for p in patterns) + NA_KROWS - 1 + NA_ROWS - 1
    pad_lo, pad_hi = max(0, -lo), max(0, hi - (2 * NA_ROWS - 2))
    by_col = jnp.pad(by_col, ((0, 0), (pad_lo, pad_hi), (0, 0), (0, 0)))
    tiles, masks = [], []
    for kstart_rel, r0_rel in patterns:
        per_qrow = []
        for a in range(NA_QROWS):
            first = kstart_rel - a + NA_ROWS - 1 + pad_lo
            per_qrow.append(by_col[:, first:first + NA_KROWS])
        tiles.append(jnp.stack(per_qrow, 1))
        r0 = np.asarray(r0_rel)[qa]
        krel = kstart_rel + ka - qa
        masks.append((krel >= r0) & (krel < r0 + kr) & col_ok)
    bias = jnp.transpose(jnp.stack(tiles, 1), (0, 1, 2, 4, 3, 5))
    bias = bias.reshape(D_HEADS, len(patterns), NA_QROWS * GRID_W, NA_KROWS * GRID_W)
    bias = jnp.where(np.stack(masks)[None], bias * LOG2E, NEG_INF)
    return bias.reshape(D_HEADS * len(patterns), NA_QROWS * GRID_W, NA_KROWS * GRID_W)


def _na_kernel(q_ref, k_ref, v_ref, b_ref, o_ref, qa, qb, va, vb, s0, s1, s2, s3, p0, p1, p2, p3,
               i0, i1, i2, i3, *, rows):
    s_buf, p_buf, inv_buf = (s0, s1, s2, s3), (p0, p1, p2, p3), (i0, i1, i2, i3)
    kstart, pattern, patterns = _na_geometry(rows)
    npat = len(patterns)
    nq = NA_QROWS * GRID_W
    nk = NA_KROWS * GRID_W
    for c in range(rows // NA_QROWS):
        rs = slice(nq * c, nq * (c + 1))
        head0 = lax.broadcasted_iota(jnp.int32, (nq, LANES), 1) < HEAD_DIM
        t = q_ref[0, rs, :].astype(F32)
        qa[rs, :] = jnp.where(head0, t, 0.0).astype(BF16)
        qb[rs, :] = jnp.where(head0, 0.0, t).astype(BF16)
        t = v_ref[0, rs, :].astype(F32)
        va[rs, :] = jnp.where(head0, t, 0.0).astype(BF16)
        vb[rs, :] = jnp.where(head0, 0.0, t).astype(BF16)
    heads = ((qa, va), (qb, vb))

    def window(j):
        return slice(nq * j, nq * (j + 1)), slice(GRID_W * kstart[j], GRID_W * kstart[j] + nk)

    def scores(j, slot):
        qs, ks = window(j)
        kw = k_ref[0, ks, :]
        for h in range(2):
            s_buf[2 * slot + h][...] = _dot_nt(heads[h][0][qs, :], kw) + b_ref[h * npat + pattern[j]]

    def softmax(j, slot):
        for h in range(2):
            s = s_buf[2 * slot + h][...]
            m = jnp.max(s, -1, keepdims=True)
            p = jnp.exp2(s - m)
            p_buf[2 * slot + h][...] = p.astype(BF16)
            inv_buf[2 * slot + h][...] = jnp.broadcast_to(1.0 / jnp.sum(p, -1, keepdims=True), (nq, LANES))

    def values(j, slot):
        qs, ks = window(j)
        out = None
        for h in range(2):
            o = _dot(p_buf[2 * slot + h][...], heads[h][1][ks, :]) * inv_buf[2 * slot + h][...]
            out = o if out is None else out + o
        o_ref[0, qs, :] = out.astype(BF16)

    _pipeline3(rows // NA_QROWS, scores, softmax, values)


def _neighbourhood(qd, kd, vd, bias):
    b, s, _ = qd.shape
    npair = D_HEADS // 2
    tok = pl.BlockSpec((1, s, LANES), lambda hp, bi: (bi, 0, hp))
    nb = bias.shape[0] // npair
    return pl.pallas_call(
        functools.partial(_na_kernel, rows=s // GRID_W),
        grid=(npair, b),
        in_specs=[tok, tok, tok, pl.BlockSpec((nb,) + bias.shape[1:], lambda hp, bi: (hp, 0, 0))],
        out_specs=tok,
        out_shape=jax.ShapeDtypeStruct((b, s, D_WIDTH), BF16),
        scratch_shapes=([pltpu.VMEM((s, LANES), BF16)] * 4
                        + [pltpu.VMEM((NA_QROWS * GRID_W, NA_KROWS * GRID_W), F32)] * 4
                        + [pltpu.VMEM((NA_QROWS * GRID_W, NA_KROWS * GRID_W), BF16)] * 4
                        + [pltpu.VMEM((NA_QROWS * GRID_W, LANES), F32)] * 4),
        compiler_params=_params(("parallel", "arbitrary")),
        name="neighbourhood_attn",
    )(qd, kd, vd, bias)


def _mix_ffn_kernel(x_ref, a_ref, b_ref, wo_ref, g_ref, wgu_ref, wd_ref, fg_ref, o_ref, *, final, nchunk):
    half = a_ref.shape[1]
    x = x_ref[...] + _dot(a_ref[...], wo_ref[0:half, :]) + _dot(b_ref[...], wo_ref[half:2 * half, :])
    h = _rms(x, g_ref[...]).astype(BF16)
    hidden = wd_ref.shape[0]
    hc = hidden // nchunk
    acc = x
    for c in range(nchunk):
        gate = _dot(h, wgu_ref[:, hc * c:hc * (c + 1)])
        up = _dot(h, wgu_ref[:, hidden + hc * c:hidden + hc * (c + 1)])
        act = (gate * (1.0 / (1.0 + jnp.exp(-gate))) * up).astype(BF16)
        acc = acc + _dot(act, wd_ref[hc * c:hc * (c + 1), :])
    o_ref[...] = _rms(acc, fg_ref[...]) if final else acc


def _mix_ffn(x2, a, b, wo, g, wgu, wd, fg, final):
    t, d = x2.shape
    tm = ROW_TILE
    row = lambda n: pl.BlockSpec((tm, n), lambda i: (i, 0))
    return pl.pallas_call(
        functools.partial(_mix_ffn_kernel, final=final, nchunk=1),
        grid=(t // tm,),
        in_specs=[row(d), row(a.shape[1]), row(b.shape[1]), _const_spec(wo.shape), _const_spec(g.shape),
                  _const_spec(wgu.shape), _const_spec(wd.shape), _const_spec(fg.shape)],
        out_specs=row(d),
        out_shape=jax.ShapeDtypeStruct((t, d), F32),
        compiler_params=_params(("parallel",)),
        name="mix_ffn",
    )(x2, a, b, wo, g, wgu, wd, fg)


def _rope_tables(seq):
    pos = jnp.arange(seq, dtype=F32)
    inv = 1.0 / (ROPE_BASE ** (jnp.arange(0, B_ROPE, 2, dtype=F32) / B_ROPE))
    ang = pos[:, None] * inv[None, :]
    cos, sin = jnp.cos(ang), jnp.sin(ang)
    hf = B_ROPE // 2
    ones = jnp.ones((seq, B_NOPE), F32)
    z = lambda n: jnp.zeros((seq, n), F32)
    tail = LANES - B_NOPE - B_ROPE
    c = jnp.concatenate([ones, cos, cos, z(tail)], 1)
    s1 = jnp.concatenate([z(B_NOPE), -sin, z(hf), z(tail)], 1)
    s2 = jnp.concatenate([z(B_NOPE), z(hf), sin, z(tail)], 1)
    return c, s1, s2


def _pad_heads(w, heads, width):
    k = w.shape[0]
    return jnp.pad(w.reshape(k, heads, width), ((0, 0), (0, 0), (0, LANES - width))).reshape(k, heads * LANES)


def kernel(x, t5_bias, norm_mix, norm_ffn, ev_w_in, ev_q_gain, ev_kv_gain, ev_w_uq, ev_w_ukv, ev_w_out,
           od_w_in, od_v_gain, od_w_s, od_b_s, od_rpb, od_w_out, ffn_w_gu, ffn_w_down, final_gain):
    bsz, seq, d = x.shape
    depth = norm_mix.shape[0]
    t = bsz * seq
    assert seq % ROW_TILE == 0 and seq % GRID_W == 0 and ROW_TILE % C_CHUNK == 0
    x2 = x.reshape(t, d)
    rope_c, rope_s1, rope_s2 = _rope_tables(seq)
    dil_bias = [_dil_bias(t5_bias, seq, dil) for _, dil in A_BRANCHES]
    row = lambda v: v.reshape(1, -1).astype(F32)
    tok3 = lambda a: a.reshape(bsz, seq, a.shape[-1])
    tok2 = lambda a: a.reshape(t, a.shape[-1])

    for layer in range(depth):
        j = layer // 2
        if layer % 2 == 0:
            w_in = ev_w_in[j]
            o2 = 3 * A_WIDTH + B_Q_RANK + B_KV_RANK
            w_in = jnp.concatenate([w_in[:, :o2], jnp.zeros((d, B_NOPE), F32), w_in[:, o2:],
                                    jnp.zeros((d, LANES - B_NOPE - B_ROPE), F32)], 1).astype(BF16)
            wuq = _pad_heads(ev_w_uq[j], B_HEADS, B_NOPE + B_ROPE).astype(BF16)
            wukv = ev_w_ukv[j].reshape(B_KV_RANK, B_HEADS, B_NOPE + B_V)
            wk = _pad_heads(wukv[:, :, :B_NOPE].reshape(B_KV_RANK, -1), B_HEADS, B_NOPE).astype(BF16)
            wv = wukv[:, :, B_NOPE:].reshape(B_KV_RANK, -1).astype(BF16)
            qa, ka, va, q, k, v = _even_in(x2, row(norm_mix[layer]), w_in, row(ev_q_gain[j]), row(ev_kv_gain[j]),
                                           wuq, wk, wv, rope_c, rope_s1, rope_s2, seq)
            m0 = tok2(_dilated(tok3(qa), tok3(ka), tok3(va), dil_bias))
            m1 = tok2(_mla(tok3(q), tok3(k), tok3(v)))
            wo = ev_w_out[j]
        else:
            bs_tile = jnp.repeat(jnp.transpose(od_b_s[j]), C_GROUP_W, axis=1).astype(F32)
            m0, qd, kd, vd = _odd_in(x2, row(norm_mix[layer]), od_w_in[j].astype(BF16), row(od_v_gain[j]),
                                     od_w_s[j].astype(BF16), bs_tile)
            m1 = tok2(_neighbourhood(tok3(qd), tok3(kd), tok3(vd), _na_bias(od_rpb[j], seq // GRID_W)))
            wo = od_w_out[j]
        x2 = _mix_ffn(x2, m0, m1, wo.astype(BF16), row(norm_ffn[layer]), ffn_w_gu[layer].astype(BF16),
                      ffn_w_down[layer].astype(BF16), row(final_gain), final=layer == depth - 1)
    return x2.reshape(bsz, seq, d)
```

```python
import functools
import math

import jax
import jax.numpy as jnp
import numpy as np
from jax import lax
from jax.experimental import pallas as pl
from jax.experimental.pallas import tpu as pltpu

HEAD_DIM = 64
EPS = 1e-6
NEG_INF = -1e30
LOG2E = math.log2(math.e)
A_HEADS = 8
A_BRANCHES = ((128, 1), (512, 4), (2048, 16))
T5_BUCKETS = 32
T5_MAX_DIST = 1024
B_HEADS = 8
B_Q_RANK = 768
B_KV_RANK = 256
B_NOPE = 64
B_ROPE = 32
B_V = 64
ROPE_BASE = 10000.0
C_GROUPS = 8
C_GROUP_W = 64
C_CHUNK = 128
C_WIDTH = C_GROUPS * C_GROUP_W
D_HEADS = 8
GRID_W = 64
NA_ROWS = 8
NA_COLS = 16
A_WIDTH = A_HEADS * HEAD_DIM
D_WIDTH = D_HEADS * HEAD_DIM

LANES = 128
VMEM_LIMIT = 56 * 1024 * 1024
ROW_TILE = 512
MLA_Q_TILE = 512
DIL_Q = 128
DIL_HALF = 64
DIL_GROUP = 1
NA_QROWS = 4
NA_KROWS = 12

BF16 = jnp.bfloat16
F32 = jnp.float32


def _dot(a, b):
    return jnp.dot(a, b, preferred_element_type=F32)


def _dot_nt(a, b):
    return lax.dot_general(a, b, (((1,), (1,)), ((), ())), preferred_element_type=F32)


def _rms(x, g):
    return x * lax.rsqrt(jnp.mean(x * x, -1, keepdims=True) + EPS) * g


def _const_spec(shape):
    nd = len(shape)
    return pl.BlockSpec(shape, lambda *_: (0,) * nd, pipeline_mode=pl.Buffered(1))


def _params(sem):
    return pltpu.CompilerParams(dimension_semantics=sem, vmem_limit_bytes=VMEM_LIMIT)


def _pipeline3(njobs, stage_a, stage_b, stage_c):
    for step in range(njobs + 2):
        for lag, stage in enumerate((stage_a, stage_b, stage_c)):
            j = step - lag
            if 0 <= j < njobs:
                stage(j, j % 2)


def _rope(t, c, s1, s2):
    return t * c + pltpu.roll(t, LANES - B_ROPE // 2, 1) * s1 + pltpu.roll(t, B_ROPE // 2, 1) * s2


def _even_in_kernel(x_ref, g_ref, w_ref, qg_ref, kvg_ref, wuq_ref, wk_ref, wv_ref, c_ref, s1_ref, s2_ref,
                    qa_ref, ka_ref, va_ref, q_ref, k_ref, v_ref):
    h = _rms(x_ref[...], g_ref[...]).astype(BF16)
    aw = A_WIDTH
    o0 = 3 * aw
    o1 = o0 + B_Q_RANK
    o2 = o1 + B_KV_RANK
    cq = _rms(_dot(h, w_ref[:, o0:o1]), qg_ref[...]).astype(BF16)
    ckv = _rms(_dot(h, w_ref[:, o1:o2]), kvg_ref[...]).astype(BF16)
    c, s1, s2 = c_ref[...], s1_ref[...], s2_ref[...]
    kpe = _rope(_dot(h, w_ref[:, o2:o2 + LANES]), c, s1, s2)
    qa_ref[...] = _dot(h, w_ref[:, 0:aw]).astype(BF16)
    q = _dot(cq, wuq_ref[...])
    kn = _dot(ckv, wk_ref[...])
    v_ref[...] = _dot(ckv, wv_ref[...]).astype(BF16)
    ka_ref[...] = _dot(h, w_ref[:, aw:2 * aw]).astype(BF16)
    qscale = (B_NOPE + B_ROPE) ** -0.5 * LOG2E
    for hh in range(B_HEADS):
        sl = slice(LANES * hh, LANES * (hh + 1))
        q_ref[:, sl] = (_rope(q[:, sl], c, s1, s2) * qscale).astype(BF16)
        k_ref[:, sl] = (kn[:, sl] + kpe).astype(BF16)
    va_ref[...] = _dot(h, w_ref[:, 2 * aw:3 * aw]).astype(BF16)


def _even_in(x2, g, w_in, qg, kvg, wuq, wk, wv, rope_c, rope_s1, rope_s2, seq):
    t, d = x2.shape
    tm = ROW_TILE
    nseq = seq // tm
    row = lambda n: pl.BlockSpec((tm, n), lambda i: (i, 0))
    pos = pl.BlockSpec((tm, LANES), lambda i: (i % nseq, 0))
    qk_w = B_HEADS * LANES
    out_shapes = [jax.ShapeDtypeStruct((t, n), BF16) for n in (A_WIDTH, A_WIDTH, A_WIDTH, qk_w, qk_w, B_HEADS * B_V)]
    return pl.pallas_call(
        _even_in_kernel,
        grid=(t // tm,),
        in_specs=[row(d), _const_spec(g.shape), _const_spec(w_in.shape), _const_spec(qg.shape),
                  _const_spec(kvg.shape), _const_spec(wuq.shape), _const_spec(wk.shape), _const_spec(wv.shape),
                  pos, pos, pos],
        out_specs=[row(s.shape[1]) for s in out_shapes],
        out_shape=out_shapes,
        compiler_params=_params(("parallel",)),
        name="even_in",
    )(x2, g, w_in, qg, kvg, wuq, wk, wv, rope_c, rope_s1, rope_s2)


def _mla_kernel(q_ref, k_ref, v_ref, o_ref, s_a, s_b, p_a, p_b, inv_a, inv_b, va, vb, acc):
    seq = q_ref.shape[1]
    sub = MLA_Q_TILE
    head0 = lax.broadcasted_iota(jnp.int32, (sub, LANES), 1) < B_V
    for t in range(seq // sub):
        rs = slice(sub * t, sub * (t + 1))
        x = v_ref[0, rs, :].astype(F32)
        va[rs, :] = jnp.where(head0, x, 0.0).astype(BF16)
        vb[rs, :] = jnp.where(head0, 0.0, x).astype(BF16)
    jobs = [(t, h) for t in range(seq // sub) for h in range(2)]
    s_buf, p_buf, inv_buf = (s_a, s_b), (p_a, p_b), (inv_a, inv_b)

    def scores(j, slot):
        t, h = jobs[j]
        sl = slice(LANES * h, LANES * (h + 1))
        s_buf[slot][...] = _dot_nt(q_ref[0, sub * t:sub * (t + 1), sl], k_ref[0, :, sl])

    def softmax(j, slot):
        s = s_buf[slot][...]
        m = jnp.max(s, -1, keepdims=True)
        p = jnp.exp2(s - m)
        p_buf[slot][...] = p.astype(BF16)
        inv_buf[slot][...] = jnp.broadcast_to(1.0 / jnp.sum(p, -1, keepdims=True), (sub, LANES))

    def values(j, slot):
        t, h = jobs[j]
        o = _dot(p_buf[slot][...], (va, vb)[h][...]) * inv_buf[slot][...]
        if h == 0:
            acc[...] = o
        else:
            o_ref[0, sub * t:sub * (t + 1), :] = (acc[...] + o).astype(BF16)

    _pipeline3(len(jobs), scores, softmax, values)


def _mla(q, k, v):
    b, s, _ = q.shape
    sub = MLA_Q_TILE
    return pl.pallas_call(
        _mla_kernel,
        grid=(b, B_HEADS // 2),
        in_specs=[pl.BlockSpec((1, s, 2 * LANES), lambda bi, hp: (bi, 0, hp)),
                  pl.BlockSpec((1, s, 2 * LANES), lambda bi, hp: (bi, 0, hp)),
                  pl.BlockSpec((1, s, LANES), lambda bi, hp: (bi, 0, hp))],
        out_specs=pl.BlockSpec((1, s, LANES), lambda bi, hp: (bi, 0, hp)),
        out_shape=jax.ShapeDtypeStruct((b, s, B_HEADS * B_V), BF16),
        scratch_shapes=([pltpu.VMEM((sub, s), F32)] * 2 + [pltpu.VMEM((sub, s), BF16)] * 2
                        + [pltpu.VMEM((sub, LANES), F32)] * 2 + [pltpu.VMEM((s, LANES), BF16)] * 2
                        + [pltpu.VMEM((sub, LANES), F32)]),
        compiler_params=_params(("parallel", "parallel")),
        name="mla_attn",
    )(q, k, v)


def _t5_bucket(rel):
    nb = T5_BUCKETS // 2
    max_exact = nb // 2
    n = np.abs(rel)
    large = max_exact + (np.log(np.maximum(n, 1) / max_exact) / math.log(T5_MAX_DIST / max_exact)
                         * (nb - max_exact)).astype(np.int64)
    large = np.minimum(large, nb - 1)
    return ((rel > 0) * nb + np.where(n < max_exact, n, large)).astype(np.int32)


def _dil_geometry(seq, dil):
    length = seq // dil
    assert length % DIL_Q == 0
    if length == DIL_Q:
        return length, DIL_Q, (0,)
    return length, 2 * DIL_Q, (0, DIL_HALF, DIL_Q)


def _toeplitz(vec, nrow, ncol):
    p = vec.shape[-1]
    flat = jnp.tile(vec, (1,) * (vec.ndim - 1) + (nrow,))[..., :nrow * (p - 1)]
    return flat.reshape(vec.shape[:-1] + (nrow, p - 1))[..., :ncol]


def _dil_bias(t5_bias, seq, dil):
    _, kwid, shifts = _dil_geometry(seq, dil)
    period = 2 * (DIL_Q + kwid)
    k = np.arange(period)
    k = np.where(k < kwid, k, k - period)
    off = k[None, :] - np.asarray(shifts)[:, None]
    valid = np.abs(off) <= DIL_HALF
    vec = jnp.transpose(t5_bias[_t5_bucket(off * dil)], (2, 0, 1)).astype(F32)
    vec = jnp.where(valid[None], vec * LOG2E, NEG_INF)
    return _toeplitz(vec, DIL_Q, kwid).reshape(A_HEADS * len(shifts), DIL_Q, kwid)


def _dil_kernel(q_ref, k_ref, v_ref, b1_ref, b4_ref, b16_ref, o_ref,
                f0, f1, f2, f3, f4, f5,
                qa1, qb1, va1, vb1, qa4, qb4, k4, va4, vb4, qa16, qb16, k16, va16, vb16,
                ob1, ob4, ob16, lb1, lb4, lb16, s0, s1, s2, s3, p0, p1, p2, p3, i0, i1, *, seq):
    s_buf, p_buf, inv_buf = (s0, s1, s2, s3), (p0, p1, p2, p3), (i0, i1)
    assert tuple(d for _, d in A_BRANCHES) == (1, 4, 16)
    qscale = HEAD_DIM ** -0.5 * LOG2E
    len4, len16 = seq // 4, seq // 16

    def split_heads(t, scale=None):
        head0 = lax.broadcasted_iota(jnp.int32, t.shape, 1) < HEAD_DIM
        if scale is not None:
            t = t * scale
        return jnp.where(head0, t, 0.0).astype(BF16), jnp.where(head0, 0.0, t).astype(BF16)

    for c in range(4):
        rs = slice(len4 * c, len4 * (c + 1))
        t = q_ref[0, rs, :].astype(F32)
        f0[rs, :] = t
        qa1[rs, :], qb1[rs, :] = split_heads(t, qscale)
        f1[rs, :] = k_ref[0, rs, :].astype(F32)
        t = v_ref[0, rs, :].astype(F32)
        f2[rs, :] = t
        va1[rs, :], vb1[rs, :] = split_heads(t)
    for r in range(4):
        rs = slice(len4 * r, len4 * (r + 1))
        t = f0[pl.ds(r, len4, stride=4), :]
        f3[rs, :] = t
        qa4[rs, :], qb4[rs, :] = split_heads(t, qscale)
        t = f1[pl.ds(r, len4, stride=4), :]
        f4[rs, :] = t
        k4[rs, :] = t.astype(BF16)
        t = f2[pl.ds(r, len4, stride=4), :]
        f5[rs, :] = t
        va4[rs, :], vb4[rs, :] = split_heads(t)
    for r4 in range(4):
        for rp in range(4):
            rs = slice(len16 * (r4 + 4 * rp), len16 * (r4 + 4 * rp + 1))
            src = pl.ds(len4 * r4 + rp, len16, stride=4)
            qa16[rs, :], qb16[rs, :] = split_heads(f3[src, :], qscale)
            k16[rs, :] = f4[src, :].astype(BF16)
            va16[rs, :], vb16[rs, :] = split_heads(f5[src, :])

    branches = ((1, (qa1, qb1), lambda ks: k_ref[0, ks, :], (va1, vb1), b1_ref, ob1, lb1),
                (4, (qa4, qb4), lambda ks: k4[ks, :], (va4, vb4), b4_ref, ob4, lb4),
                (16, (qa16, qb16), lambda ks: k16[ks, :], (va16, vb16), b16_ref, ob16, lb16))
    ngroup = seq // (DIL_Q * DIL_GROUP)
    grows = DIL_Q * DIL_GROUP
    head0 = lax.broadcasted_iota(jnp.int32, (grows, LANES), 1) < HEAD_DIM

    def geom(job):
        dil = branches[job // ngroup][0]
        length, kwid, shifts = _dil_geometry(seq, dil)
        nqb = length // DIL_Q
        units = []
        for i in range(DIL_GROUP):
            u = (job % ngroup) * DIL_GROUP + i
            seg, qb = divmod(u, nqb)
            kstart = seg * length + min(max(qb * DIL_Q - DIL_HALF, 0), length - kwid)
            case = 0 if len(shifts) == 1 else (0 if qb == 0 else 2 if qb == nqb - 1 else 1)
            if dil == 1:
                out = slice(DIL_Q * u, DIL_Q * (u + 1))
            elif dil == 4:
                out = pl.ds(seg + dil * DIL_Q * qb, DIL_Q, stride=4)
            else:
                out = pl.ds(len4 * (u % 4) + u // 4, DIL_Q, stride=4)
            units.append((slice(DIL_Q * u, DIL_Q * (u + 1)), slice(DIL_Q * i, DIL_Q * (i + 1)),
                          slice(kstart, kstart + kwid), case, out))
        return kwid, len(shifts), units

    def scores(job, slot):
        _, q_pair, kload, _, b_ref, _, _ = branches[job // ngroup]
        kwid, ncase, units = geom(job)
        for qs, js, ks, case, _ in units:
            kw = kload(ks)
            for h in range(2):
                s_buf[2 * slot + h][js, 0:kwid] = _dot_nt(q_pair[h][qs, :], kw) + b_ref[h * ncase + case]

    def softmax(job, slot):
        lb = branches[job // ngroup][6]
        kwid, _, units = geom(job)
        stats = []
        for h in range(2):
            s = s_buf[2 * slot + h][:, 0:kwid]
            m = jnp.max(s, -1, keepdims=True)
            p = jnp.exp2(s - m)
            p_buf[2 * slot + h][:, 0:kwid] = p.astype(BF16)
            stats.append((m, jnp.sum(p, -1, keepdims=True)))
        den = jnp.where(head0, stats[0][1], stats[1][1])
        inv_buf[slot][...] = 1.0 / den
        lse = jnp.where(head0, stats[0][0], stats[1][0]) + jnp.log2(den)
        for _, js, _, _, out in units:
            lb[out, :] = lse[js, :]

    def values(job, slot):
        _, _, _, v_pair, _, ob, _ = branches[job // ngroup]
        kwid, _, units = geom(job)
        for _, js, ks, _, out in units:
            o = (_dot(p_buf[2 * slot][js, 0:kwid], v_pair[0][ks, :])
                 + _dot(p_buf[2 * slot + 1][js, 0:kwid], v_pair[1][ks, :]))
            ob[out, :] = o * inv_buf[slot][js, :]

    _pipeline3(len(branches) * ngroup, scores, softmax, values)

    for src16, dst in ((ob16, f0), (lb16, f1)):
        for r in range(4):
            dst[pl.ds(r, len4, stride=4), :] = src16[len4 * r:len4 * (r + 1), :]

    def combine(c, carry):
        sl = pl.ds(pl.multiple_of(c * 256, 256), 256)
        l0, l1, l2 = lb1[sl, :], lb4[sl, :], f1[sl, :]
        m = jnp.maximum(jnp.maximum(l0, l1), l2)
        w0, w1, w2 = jnp.exp2(l0 - m), jnp.exp2(l1 - m), jnp.exp2(l2 - m)
        mix = (w0 * ob1[sl, :] + w1 * ob4[sl, :] + w2 * f0[sl, :]) / (w0 + w1 + w2)
        o_ref[0, sl, :] = mix.astype(BF16)
        return carry

    lax.fori_loop(0, seq // 256, combine, 0)


def _dilated(qa, ka, va, biases):
    b, s, _ = qa.shape
    npair = A_HEADS // 2
    tok = pl.BlockSpec((1, s, LANES), lambda hp, bi: (bi, 0, hp))

    def bias_spec(arr):
        n = arr.shape[0] // npair
        return pl.BlockSpec((n,) + arr.shape[1:], lambda hp, bi: (hp, 0, 0))

    return pl.pallas_call(
        functools.partial(_dil_kernel, seq=s),
        grid=(npair, b),
        in_specs=[tok, tok, tok] + [bias_spec(a) for a in biases],
        out_specs=tok,
        out_shape=jax.ShapeDtypeStruct((b, s, A_WIDTH), BF16),
        scratch_shapes=([pltpu.VMEM((s, LANES), F32)] * 6 + [pltpu.VMEM((s, LANES), BF16)] * 14
                        + [pltpu.VMEM((s, LANES), F32)] * 6
                        + [pltpu.VMEM((DIL_Q * DIL_GROUP, 2 * DIL_Q), F32)] * 4
                        + [pltpu.VMEM((DIL_Q * DIL_GROUP, 2 * DIL_Q), BF16)] * 4
                        + [pltpu.VMEM((DIL_Q * DIL_GROUP, LANES), F32)] * 2),
        compiler_params=_params(("parallel", "arbitrary")),
        name="dilated_attn",
    )(qa, ka, va, *biases)


def _gelu(x):
    c = math.sqrt(2.0 / math.pi)
    return x * (0.5 * (1.0 + jnp.tanh(c * (x + 0.044715 * (x * x * x)))))


def _odd_in_kernel(x_ref, g_ref, w_ref, vg_ref, ws_ref, bs_ref, c_ref, qd_ref, kd_ref, vd_ref):
    h = _rms(x_ref[...], g_ref[...]).astype(BF16)
    base = 2 * C_WIDTH
    v = _gelu(_dot(h, w_ref[:, C_WIDTH:2 * C_WIDTH]))
    u = _gelu(_dot(h, w_ref[:, 0:C_WIDTH]))
    qd_ref[...] = (_dot(h, w_ref[:, base:base + D_WIDTH]) * (HEAD_DIM ** -0.5 * LOG2E)).astype(BF16)
    kd_ref[...] = _dot(h, w_ref[:, base + D_WIDTH:base + 2 * D_WIDTH]).astype(BF16)
    vd_ref[...] = _dot(h, w_ref[:, base + 2 * D_WIDTH:base + 3 * D_WIDTH]).astype(BF16)
    mu = jnp.mean(v, -1, keepdims=True)
    vc = v - mu
    var = jnp.mean(vc * vc, -1, keepdims=True)
    vn = (vc * lax.rsqrt(var + EPS) * vg_ref[...]).astype(BF16)
    lane = lax.broadcasted_iota(jnp.int32, (C_CHUNK, LANES), 1)
    group0 = lane < C_GROUP_W
    for c in range(x_ref.shape[0] // C_CHUNK):
        rs = slice(C_CHUNK * c, C_CHUNK * (c + 1))
        for pair in range(C_GROUPS // 2):
            ls = slice(LANES * pair, LANES * (pair + 1))
            vp = vn[rs, ls]
            sv = jnp.where(group0, _dot(ws_ref[2 * pair], vp), _dot(ws_ref[2 * pair + 1], vp)) + bs_ref[:, ls]
            c_ref[rs, ls] = (u[rs, ls] * sv).astype(BF16)


def _odd_in(x2, g, w_in, vg, ws, bs_tile):
    t, d = x2.shape
    tm = ROW_TILE
    row = lambda n: pl.BlockSpec((tm, n), lambda i: (i, 0))
    out_shapes = [jax.ShapeDtypeStruct((t, n), BF16) for n in (C_WIDTH, D_WIDTH, D_WIDTH, D_WIDTH)]
    return pl.pallas_call(
        _odd_in_kernel,
        grid=(t // tm,),
        in_specs=[row(d), _const_spec(g.shape), _const_spec(w_in.shape), _const_spec(vg.shape),
                  _const_spec(ws.shape), _const_spec(bs_tile.shape)],
        out_specs=[row(s.shape[1]) for s in out_shapes],
        out_shape=out_shapes,
        compiler_params=_params(("parallel",)),
        name="odd_in",
    )(x2, g, w_in, vg, ws, bs_tile)


def _na_geometry(rows):
    kr = min(NA_ROWS, rows)
    nunit = rows // NA_QROWS
    kstart = [int(np.clip(NA_QROWS * j - kr // 2, 0, rows - NA_KROWS)) for j in range(nunit)]
    keys = [(kstart[j] - NA_QROWS * j,
             tuple(int(np.clip(i - kr // 2, 0, rows - kr)) - i for i in range(NA_QROWS * j, NA_QROWS * (j + 1))))
            for j in range(nunit)]
    patterns = sorted(set(keys), key=keys.index)
    return kstart, [patterns.index(k) for k in keys], patterns


def _na_rows(rows):
    _, _, patterns = _na_geometry(rows)
    lo = min(p[0] for p in patterns) - (NA_QROWS - 1) + NA_ROWS - 1
    hi = max(p[0] for p in patterns) + NA_KROWS - 1 + NA_ROWS - 1
    return min(lo, 0), max(hi, 2 * NA_ROWS - 2)


def _na_bias(rpb, rows):
    lo, hi = _na_rows(rows)
    qc = np.arange(GRID_W)[:, None]
    kc = np.arange(GRID_W)[None, :]
    cstart = np.clip(qc - NA_COLS // 2, 0, GRID_W - NA_COLS)
    col_ok = (kc >= cstart) & (kc < cstart + NA_COLS)
    rpb = rpb.astype(F32) * LOG2E
    gap = jnp.zeros(rpb.shape[:-1] + (2 * GRID_W - (2 * NA_COLS - 1),), F32)
    by_col = _toeplitz(jnp.concatenate([rpb[..., NA_COLS - 1:], gap, rpb[..., :NA_COLS - 1]], -1), GRID_W, GRID_W)
    by_col = jnp.where(col_ok[None, None], by_col, NEG_INF)
    by_col = jnp.pad(by_col, ((0, 0), (-lo, hi - (2 * NA_ROWS - 2)), (0, 0), (0, 0)))
    return jnp.concatenate([by_col[:, :-1], by_col[:, 1:]], -1)


def _na_kernel(q_ref, k_ref, v_ref, b_ref, o_ref, qa, qb, va, vb, s0, s1, s2, s3, p0, p1, p2, p3,
               i0, i1, i2, i3, *, rows):
    s_buf, p_buf, inv_buf = (s0, s1, s2, s3), (p0, p1, p2, p3), (i0, i1, i2, i3)
    kstart, pattern, patterns = _na_geometry(rows)
    nq = NA_QROWS * GRID_W
    nk = NA_KROWS * GRID_W
    assert 2 * GRID_W == LANES and NA_KROWS % 2 == 0
    for c in range(rows // NA_QROWS):
        rs = slice(nq * c, nq * (c + 1))
        head0 = lax.broadcasted_iota(jnp.int32, (nq, LANES), 1) < HEAD_DIM
        t = q_ref[0, rs, :].astype(F32)
        qa[rs, :] = jnp.where(head0, t, 0.0).astype(BF16)
        qb[rs, :] = jnp.where(head0, 0.0, t).astype(BF16)
        t = v_ref[0, rs, :].astype(F32)
        va[rs, :] = jnp.where(head0, t, 0.0).astype(BF16)
        vb[rs, :] = jnp.where(head0, 0.0, t).astype(BF16)
    heads = ((qa, va), (qb, vb))

    def window(j):
        return slice(nq * j, nq * (j + 1)), slice(GRID_W * kstart[j], GRID_W * kstart[j] + nk)

    lo, _ = _na_rows(rows)
    kr = min(NA_ROWS, rows)
    low_half = lax.broadcasted_iota(jnp.int32, (GRID_W, LANES), 1) < GRID_W

    def bias_tile(h, j):
        kstart_rel, r0_rel = patterns[pattern[j]]
        tile_rows = []
        for a in range(NA_QROWS):
            blocks = []
            for ka in range(0, NA_KROWS, 2):
                krel = kstart_rel + ka - a
                ok = [r0_rel[a] <= krel + i < r0_rel[a] + kr for i in range(2)]
                if not any(ok):
                    blocks.append(jnp.full((GRID_W, LANES), NEG_INF, F32))
                    continue
                blk = b_ref[h, krel + NA_ROWS - 1 - lo]
                if not ok[0]:
                    blk = jnp.where(low_half, NEG_INF, blk)
                if not ok[1]:
                    blk = jnp.where(low_half, blk, NEG_INF)
                blocks.append(blk)
            tile_rows.append(jnp.concatenate(blocks, axis=1))
        return jnp.concatenate(tile_rows, axis=0)

    def scores(j, slot):
        qs, ks = window(j)
        kw = k_ref[0, ks, :]
        for h in range(2):
            s_buf[2 * slot + h][...] = _dot_nt(heads[h][0][qs, :], kw) + bias_tile(h, j)

    def softmax(j, slot):
        for h in range(2):
            s = s_buf[2 * slot + h][...]
            m = jnp.max(s, -1, keepdims=True)
            p = jnp.exp2(s - m)
            p_buf[2 * slot + h][...] = p.astype(BF16)
            inv_buf[2 * slot + h][...] = jnp.broadcast_to(1.0 / jnp.sum(p, -1, keepdims=True), (nq, LANES))

    def values(j, slot):
        qs, ks = window(j)
        out = None
        for h in range(2):
            o = _dot(p_buf[2 * slot + h][...], heads[h][1][ks, :]) * inv_buf[2 * slot + h][...]
            out = o if out is None else out + o
        o_ref[0, qs, :] = out.astype(BF16)

    _pipeline3(rows // NA_QROWS, scores, softmax, values)


def _neighbourhood(qd, kd, vd, bias):
    b, s, _ = qd.shape
    npair = D_HEADS // 2
    tok = pl.BlockSpec((1, s, LANES), lambda hp, bi: (bi, 0, hp))
    return pl.pallas_call(
        functools.partial(_na_kernel, rows=s // GRID_W),
        grid=(npair, b),
        in_specs=[tok, tok, tok, pl.BlockSpec((2,) + bias.shape[1:], lambda hp, bi: (hp, 0, 0, 0))],
        out_specs=tok,
        out_shape=jax.ShapeDtypeStruct((b, s, D_WIDTH), BF16),
        scratch_shapes=([pltpu.VMEM((s, LANES), BF16)] * 4
                        + [pltpu.VMEM((NA_QROWS * GRID_W, NA_KROWS * GRID_W), F32)] * 4
                        + [pltpu.VMEM((NA_QROWS * GRID_W, NA_KROWS * GRID_W), BF16)] * 4
                        + [pltpu.VMEM((NA_QROWS * GRID_W, LANES), F32)] * 4),
        compiler_params=_params(("parallel", "arbitrary")),
        name="neighbourhood_attn",
    )(qd, kd, vd, bias)


def _mix_ffn_kernel(x_ref, a_ref, b_ref, wo_ref, g_ref, wgu_ref, wd_ref, fg_ref, o_ref, *, final, nchunk):
    half = a_ref.shape[1]
    x = x_ref[...] + _dot(a_ref[...], wo_ref[0:half, :]) + _dot(b_ref[...], wo_ref[half:2 * half, :])
    h = _rms(x, g_ref[...]).astype(BF16)
    hidden = wd_ref.shape[0]
    hc = hidden // nchunk
    acc = x
    for c in range(nchunk):
        gate = _dot(h, wgu_ref[:, hc * c:hc * (c + 1)])
        up = _dot(h, wgu_ref[:, hidden + hc * c:hidden + hc * (c + 1)])
        act = (gate * (1.0 / (1.0 + jnp.exp(-gate))) * up).astype(BF16)
        acc = acc + _dot(act, wd_ref[hc * c:hc * (c + 1), :])
    o_ref[...] = _rms(acc, fg_ref[...]) if final else acc


def _mix_ffn(x2, a, b, wo, g, wgu, wd, fg, final):
    t, d = x2.shape
    tm = ROW_TILE
    row = lambda n: pl.BlockSpec((tm, n), lambda i: (i, 0))
    return pl.pallas_call(
        functools.partial(_mix_ffn_kernel, final=final, nchunk=1),
        grid=(t // tm,),
        in_specs=[row(d), row(a.shape[1]), row(b.shape[1]), _const_spec(wo.shape), _const_spec(g.shape),
                  _const_spec(wgu.shape), _const_spec(wd.shape), _const_spec(fg.shape)],
        out_specs=row(d),
        out_shape=jax.ShapeDtypeStruct((t, d), F32),
        compiler_params=_params(("parallel",)),
        name="mix_ffn",
    )(x2, a, b, wo, g, wgu, wd, fg)


def _rope_tables(seq):
    pos = jnp.arange(seq, dtype=F32)
    inv = 1.0 / (ROPE_BASE ** (jnp.arange(0, B_ROPE, 2, dtype=F32) / B_ROPE))
    ang = pos[:, None] * inv[None, :]
    cos, sin = jnp.cos(ang), jnp.sin(ang)
    hf = B_ROPE // 2
    ones = jnp.ones((seq, B_NOPE), F32)
    z = lambda n: jnp.zeros((seq, n), F32)
    tail = LANES - B_NOPE - B_ROPE
    c = jnp.concatenate([ones, cos, cos, z(tail)], 1)
    s1 = jnp.concatenate([z(B_NOPE), -sin, z(hf), z(tail)], 1)
    s2 = jnp.concatenate([z(B_NOPE), z(hf), sin, z(tail)], 1)
    return c, s1, s2


def _pad_heads(w, heads, width):
    k = w.shape[0]
    return jnp.pad(w.reshape(k, heads, width), ((0, 0), (0, 0), (0, LANES - width))).reshape(k, heads * LANES)


def kernel(x, t5_bias, norm_mix, norm_ffn, ev_w_in, ev_q_gain, ev_kv_gain, ev_w_uq, ev_w_ukv, ev_w_out,
           od_w_in, od_v_gain, od_w_s, od_b_s, od_rpb, od_w_out, ffn_w_gu, ffn_w_down, final_gain):
    bsz, seq, d = x.shape
    depth = norm_mix.shape[0]
    t = bsz * seq
    assert seq % ROW_TILE == 0 and seq % GRID_W == 0 and ROW_TILE % C_CHUNK == 0
    x2 = x.reshape(t, d)
    rope_c, rope_s1, rope_s2 = _rope_tables(seq)
    dil_bias = [_dil_bias(t5_bias, seq, dil) for _, dil in A_BRANCHES]
    row = lambda v: v.reshape(1, -1).astype(F32)
    tok3 = lambda a: a.reshape(bsz, seq, a.shape[-1])
    tok2 = lambda a: a.reshape(t, a.shape[-1])

    for layer in range(depth):
        j = layer // 2
        if layer % 2 == 0:
            w_in = ev_w_in[j]
            o2 = 3 * A_WIDTH + B_Q_RANK + B_KV_RANK
            w_in = jnp.concatenate([w_in[:, :o2], jnp.zeros((d, B_NOPE), F32), w_in[:, o2:],
                                    jnp.zeros((d, LANES - B_NOPE - B_ROPE), F32)], 1).astype(BF16)
            wuq = _pad_heads(ev_w_uq[j], B_HEADS, B_NOPE + B_ROPE).astype(BF16)
            wukv = ev_w_ukv[j].reshape(B_KV_RANK, B_HEADS, B_NOPE + B_V)
            wk = _pad_heads(wukv[:, :, :B_NOPE].reshape(B_KV_RANK, -1), B_HEADS, B_NOPE).astype(BF16)
            wv = wukv[:, :, B_NOPE:].reshape(B_KV_RANK, -1).astype(BF16)
            qa, ka, va, q, k, v = _even_in(x2, row(norm_mix[layer]), w_in, row(ev_q_gain[j]), row(ev_kv_gain[j]),
                                           wuq, wk, wv, rope_c, rope_s1, rope_s2, seq)
            m0 = tok2(_dilated(tok3(qa), tok3(ka), tok3(va), dil_bias))
            m1 = tok2(_mla(tok3(q), tok3(k), tok3(v)))
            wo = ev_w_out[j]
        else:
            bs_tile = jnp.repeat(jnp.transpose(od_b_s[j]), C_GROUP_W, axis=1).astype(F32)
            m0, qd, kd, vd = _odd_in(x2, row(norm_mix[layer]), od_w_in[j].astype(BF16), row(od_v_gain[j]),
                                     od_w_s[j].astype(BF16), bs_tile)
            m1 = tok2(_neighbourhood(tok3(qd), tok3(kd), tok3(vd), _na_bias(od_rpb[j], seq // GRID_W)))
            wo = od_w_out[j]
        x2 = _mix_ffn(x2, m0, m1, wo.astype(BF16), row(norm_ffn[layer]), ffn_w_gu[layer].astype(BF16),
                      ffn_w_down[layer].astype(BF16), row(final_gain), final=layer == depth - 1)
    return x2.reshape(bsz, seq, d)
```

```python
import functools
import math

import jax
import jax.numpy as jnp
import numpy as np
from jax import lax
from jax.experimental import pallas as pl
from jax.experimental.pallas import tpu as pltpu

HEAD_DIM = 64
EPS = 1e-6
NEG_INF = -1e30
LOG2E = math.log2(math.e)
A_HEADS = 8
A_BRANCHES = ((128, 1), (512, 4), (2048, 16))
T5_BUCKETS = 32
T5_MAX_DIST = 1024
B_HEADS = 8
B_Q_RANK = 768
B_KV_RANK = 256
B_NOPE = 64
B_ROPE = 32
B_V = 64
ROPE_BASE = 10000.0
C_GROUPS = 8
C_GROUP_W = 64
C_CHUNK = 128
C_WIDTH = C_GROUPS * C_GROUP_W
D_HEADS = 8
GRID_W = 64
NA_ROWS = 8
NA_COLS = 16
A_WIDTH = A_HEADS * HEAD_DIM
D_WIDTH = D_HEADS * HEAD_DIM

LANES = 128
VMEM_LIMIT = 56 * 1024 * 1024
ROW_TILE = 512
MLA_Q_TILE = 512
DIL_Q = 128
DIL_HALF = 64
DIL_GROUP = 1
NA_QROWS = 4
NA_KROWS = 12

BF16 = jnp.bfloat16
F32 = jnp.float32


def _dot(a, b):
    return jnp.dot(a, b, preferred_element_type=F32)


def _dot_nt(a, b):
    return lax.dot_general(a, b, (((1,), (1,)), ((), ())), preferred_element_type=F32)


def _rms(x, g):
    return x * lax.rsqrt(jnp.mean(x * x, -1, keepdims=True) + EPS) * g


def _const_spec(shape):
    nd = len(shape)
    return pl.BlockSpec(shape, lambda *_: (0,) * nd, pipeline_mode=pl.Buffered(1))


def _params(sem):
    return pltpu.CompilerParams(dimension_semantics=sem, vmem_limit_bytes=VMEM_LIMIT)


def _pipeline3(njobs, stage_a, stage_b, stage_c):
    for step in range(njobs + 2):
        for lag, stage in enumerate((stage_a, stage_b, stage_c)):
            j = step - lag
            if 0 <= j < njobs:
                stage(j, j % 2)


def _rope(t, c, s1, s2):
    return t * c + pltpu.roll(t, LANES - B_ROPE // 2, 1) * s1 + pltpu.roll(t, B_ROPE // 2, 1) * s2


def _even_in_kernel(x_ref, g_ref, w_ref, qg_ref, kvg_ref, wuq_ref, wk_ref, wv_ref, c_ref, s1_ref, s2_ref,
                    qa_ref, ka_ref, va_ref, q_ref, k_ref, v_ref):
    h = _rms(x_ref[...], g_ref[...]).astype(BF16)
    aw = A_WIDTH
    o0 = 3 * aw
    o1 = o0 + B_Q_RANK
    o2 = o1 + B_KV_RANK
    cq = _rms(_dot(h, w_ref[:, o0:o1]), qg_ref[...]).astype(BF16)
    ckv = _rms(_dot(h, w_ref[:, o1:o2]), kvg_ref[...]).astype(BF16)
    c, s1, s2 = c_ref[...], s1_ref[...], s2_ref[...]
    kpe = _rope(_dot(h, w_ref[:, o2:o2 + LANES]), c, s1, s2)
    qa_ref[...] = _dot(h, w_ref[:, 0:aw]).astype(BF16)
    q = _dot(cq, wuq_ref[...])
    kn = _dot(ckv, wk_ref[...])
    v_ref[...] = _dot(ckv, wv_ref[...]).astype(BF16)
    ka_ref[...] = _dot(h, w_ref[:, aw:2 * aw]).astype(BF16)
    qscale = (B_NOPE + B_ROPE) ** -0.5 * LOG2E
    for hh in range(B_HEADS):
        sl = slice(LANES * hh, LANES * (hh + 1))
        q_ref[:, sl] = (_rope(q[:, sl], c, s1, s2) * qscale).astype(BF16)
        k_ref[:, sl] = (kn[:, sl] + kpe).astype(BF16)
    va_ref[...] = _dot(h, w_ref[:, 2 * aw:3 * aw]).astype(BF16)


def _even_in(x2, g, w_in, qg, kvg, wuq, wk, wv, rope_c, rope_s1, rope_s2, seq):
    t, d = x2.shape
    tm = ROW_TILE
    nseq = seq // tm
    row = lambda n: pl.BlockSpec((tm, n), lambda i: (i, 0))
    pos = pl.BlockSpec((tm, LANES), lambda i: (i % nseq, 0))
    qk_w = B_HEADS * LANES
    out_shapes = [jax.ShapeDtypeStruct((t, n), BF16) for n in (A_WIDTH, A_WIDTH, A_WIDTH, qk_w, qk_w, B_HEADS * B_V)]
    return pl.pallas_call(
        _even_in_kernel,
        grid=(t // tm,),
        in_specs=[row(d), _const_spec(g.shape), _const_spec(w_in.shape), _const_spec(qg.shape),
                  _const_spec(kvg.shape), _const_spec(wuq.shape), _const_spec(wk.shape), _const_spec(wv.shape),
                  pos, pos, pos],
        out_specs=[row(s.shape[1]) for s in out_shapes],
        out_shape=out_shapes,
        compiler_params=_params(("parallel",)),
        name="even_in",
    )(x2, g, w_in, qg, kvg, wuq, wk, wv, rope_c, rope_s1, rope_s2)


def _mla_kernel(q_ref, k_ref, v_ref, o_ref, s_a, s_b, p_a, p_b, inv_a, inv_b, va, vb, acc):
    seq = q_ref.shape[1]
    sub = MLA_Q_TILE
    head0 = lax.broadcasted_iota(jnp.int32, (sub, LANES), 1) < B_V
    for t in range(seq // sub):
        rs = slice(sub * t, sub * (t + 1))
        x = v_ref[0, rs, :].astype(F32)
        va[rs, :] = jnp.where(head0, x, 0.0).astype(BF16)
        vb[rs, :] = jnp.where(head0, 0.0, x).astype(BF16)
    jobs = [(t, h) for t in range(seq // sub) for h in range(2)]
    s_buf, p_buf, inv_buf = (s_a, s_b), (p_a, p_b), (inv_a, inv_b)

    def scores(j, slot):
        t, h = jobs[j]
        sl = slice(LANES * h, LANES * (h + 1))
        s_buf[slot][...] = _dot_nt(q_ref[0, sub * t:sub * (t + 1), sl], k_ref[0, :, sl])

    def softmax(j, slot):
        s = s_buf[slot][...]
        m = jnp.max(s, -1, keepdims=True)
        p = jnp.exp2(s - m)
        p_buf[slot][...] = p.astype(BF16)
        inv_buf[slot][...] = jnp.broadcast_to(1.0 / jnp.sum(p, -1, keepdims=True), (sub, LANES))

    def values(j, slot):
        t, h = jobs[j]
        o = _dot(p_buf[slot][...], (va, vb)[h][...]) * inv_buf[slot][...]
        if h == 0:
            acc[...] = o
        else:
            o_ref[0, sub * t:sub * (t + 1), :] = (acc[...] + o).astype(BF16)

    _pipeline3(len(jobs), scores, softmax, values)


def _mla(q, k, v):
    b, s, _ = q.shape
    sub = MLA_Q_TILE
    return pl.pallas_call(
        _mla_kernel,
        grid=(b, B_HEADS // 2),
        in_specs=[pl.BlockSpec((1, s, 2 * LANES), lambda bi, hp: (bi, 0, hp)),
                  pl.BlockSpec((1, s, 2 * LANES), lambda bi, hp: (bi, 0, hp)),
                  pl.BlockSpec((1, s, LANES), lambda bi, hp: (bi, 0, hp))],
        out_specs=pl.BlockSpec((1, s, LANES), lambda bi, hp: (bi, 0, hp)),
        out_shape=jax.ShapeDtypeStruct((b, s, B_HEADS * B_V), BF16),
        scratch_shapes=([pltpu.VMEM((sub, s), F32)] * 2 + [pltpu.VMEM((sub, s), BF16)] * 2
                        + [pltpu.VMEM((sub, LANES), F32)] * 2 + [pltpu.VMEM((s, LANES), BF16)] * 2
                        + [pltpu.VMEM((sub, LANES), F32)]),
        compiler_params=_params(("parallel", "parallel")),
        name="mla_attn",
    )(q, k, v)


def _t5_bucket(rel):
    nb = T5_BUCKETS // 2
    max_exact = nb // 2
    n = np.abs(rel)
    large = max_exact + (np.log(np.maximum(n, 1) / max_exact) / math.log(T5_MAX_DIST / max_exact)
                         * (nb - max_exact)).astype(np.int64)
    large = np.minimum(large, nb - 1)
    return ((rel > 0) * nb + np.where(n < max_exact, n, large)).astype(np.int32)


def _dil_geometry(seq, dil):
    length = seq // dil
    assert length % DIL_Q == 0
    if length == DIL_Q:
        return length, DIL_Q, (0,)
    return length, 2 * DIL_Q, (0, DIL_HALF, DIL_Q)


def _toeplitz(vec, nrow, ncol):
    p = vec.shape[-1]
    flat = jnp.tile(vec, (1,) * (vec.ndim - 1) + (nrow,))[..., :nrow * (p - 1)]
    return flat.reshape(vec.shape[:-1] + (nrow, p - 1))[..., :ncol]


def _dil_bias(t5_bias, seq, dil):
    _, kwid, shifts = _dil_geometry(seq, dil)
    period = 2 * (DIL_Q + kwid)
    k = np.arange(period)
    k = np.where(k < kwid, k, k - period)
    off = k[None, :] - np.asarray(shifts)[:, None]
    valid = np.abs(off) <= DIL_HALF
    vec = jnp.transpose(t5_bias[_t5_bucket(off * dil)], (2, 0, 1)).astype(F32)
    vec = jnp.where(valid[None], vec * LOG2E, NEG_INF)
    bias = _toeplitz(vec, DIL_Q, kwid).reshape(A_HEADS // 2, 2, len(shifts), DIL_Q, kwid)
    return jnp.transpose(bias, (0, 2, 1, 3, 4)).reshape(A_HEADS // 2 * len(shifts), 2 * DIL_Q, kwid)


def _dil_kernel(q_ref, k_ref, v_ref, b1_ref, b4_ref, b16_ref, o_ref,
                f0, f1, f2, f3, f4, f5, q1, q4, k4, v4, q16, k16, v16,
                ob1, ob4, ob16, lb1, lb4, lb16, s0, s1, p0, p1, i0, i1, *, seq):
    s_buf, p_buf, inv_buf = (s0, s1), (p0, p1), (i0, i1)
    assert tuple(d for _, d in A_BRANCHES) == (1, 4, 16)
    qscale = HEAD_DIM ** -0.5 * LOG2E
    len4, len16 = seq // 4, seq // 16

    def put_q(dst, row0, t):
        head0 = lax.broadcasted_iota(jnp.int32, t.shape, 1) < HEAD_DIM
        t = t * qscale
        a, b = jnp.where(head0, t, 0.0).astype(BF16), jnp.where(head0, 0.0, t).astype(BF16)
        for i in range(t.shape[0] // DIL_Q):
            us = slice(DIL_Q * i, DIL_Q * (i + 1))
            dst[2 * row0 + 2 * DIL_Q * i:2 * row0 + 2 * DIL_Q * i + DIL_Q, :] = a[us, :]
            dst[2 * row0 + 2 * DIL_Q * i + DIL_Q:2 * row0 + 2 * DIL_Q * (i + 1), :] = b[us, :]

    for c in range(4):
        rs = slice(len4 * c, len4 * (c + 1))
        t = q_ref[0, rs, :].astype(F32)
        f0[rs, :] = t
        put_q(q1, len4 * c, t)
        f1[rs, :] = k_ref[0, rs, :].astype(F32)
        f2[rs, :] = v_ref[0, rs, :].astype(F32)
    for r in range(4):
        rs = slice(len4 * r, len4 * (r + 1))
        t = f0[pl.ds(r, len4, stride=4), :]
        f3[rs, :] = t
        put_q(q4, len4 * r, t)
        t = f1[pl.ds(r, len4, stride=4), :]
        f4[rs, :] = t
        k4[rs, :] = t.astype(BF16)
        t = f2[pl.ds(r, len4, stride=4), :]
        f5[rs, :] = t
        v4[rs, :] = t.astype(BF16)
    for r4 in range(4):
        for rp in range(4):
            rs = slice(len16 * (r4 + 4 * rp), len16 * (r4 + 4 * rp + 1))
            src = pl.ds(len4 * r4 + rp, len16, stride=4)
            put_q(q16, len16 * (r4 + 4 * rp), f3[src, :])
            k16[rs, :] = f4[src, :].astype(BF16)
            v16[rs, :] = f5[src, :].astype(BF16)

    branches = ((1, q1, lambda ks: k_ref[0, ks, :], lambda ks: v_ref[0, ks, :], b1_ref, ob1, lb1),
                (4, q4, lambda ks: k4[ks, :], lambda ks: v4[ks, :], b4_ref, ob4, lb4),
                (16, q16, lambda ks: k16[ks, :], lambda ks: v16[ks, :], b16_ref, ob16, lb16))
    ngroup = seq // (DIL_Q * DIL_GROUP)
    head0 = lax.broadcasted_iota(jnp.int32, (DIL_Q, LANES), 1) < HEAD_DIM

    def geom(job):
        dil = branches[job // ngroup][0]
        length, kwid, shifts = _dil_geometry(seq, dil)
        nqb = length // DIL_Q
        units = []
        for i in range(DIL_GROUP):
            u = (job % ngroup) * DIL_GROUP + i
            seg, qb = divmod(u, nqb)
            kstart = seg * length + min(max(qb * DIL_Q - DIL_HALF, 0), length - kwid)
            case = 0 if len(shifts) == 1 else (0 if qb == 0 else 2 if qb == nqb - 1 else 1)
            if dil == 1:
                out = slice(DIL_Q * u, DIL_Q * (u + 1))
            elif dil == 4:
                out = pl.ds(seg + dil * DIL_Q * qb, DIL_Q, stride=4)
            else:
                out = pl.ds(len4 * (u % 4) + u // 4, DIL_Q, stride=4)
            units.append((u, slice(kstart, kstart + kwid), case, out))
        return kwid, units

    def pair_rows(i):
        return slice(2 * DIL_Q * i, 2 * DIL_Q * (i + 1))

    def scores(job, slot):
        _, q_all, kload, _, b_ref, _, _ = branches[job // ngroup]
        kwid, units = geom(job)
        for i, (u, ks, case, _) in enumerate(units):
            s_buf[slot][pair_rows(i), 0:kwid] = _dot_nt(q_all[pair_rows(u), :], kload(ks)) + b_ref[case]

    def softmax(job, slot):
        lb = branches[job // ngroup][6]
        kwid, units = geom(job)
        s = s_buf[slot][:, 0:kwid]
        m = jnp.max(s, -1, keepdims=True)
        p = jnp.exp2(s - m)
        p_buf[slot][:, 0:kwid] = p.astype(BF16)
        den = jnp.sum(p, -1, keepdims=True)
        for i, (_, _, _, out) in enumerate(units):
            top, bot = slice(2 * DIL_Q * i, 2 * DIL_Q * i + DIL_Q), slice(2 * DIL_Q * i + DIL_Q, 2 * DIL_Q * (i + 1))
            den_c = jnp.where(head0, den[top, :], den[bot, :])
            inv_buf[slot][DIL_Q * i:DIL_Q * (i + 1), :] = 1.0 / den_c
            lb[out, :] = jnp.where(head0, m[top, :], m[bot, :]) + jnp.log2(den_c)

    def values(job, slot):
        _, _, _, vload, _, ob, _ = branches[job // ngroup]
        kwid, units = geom(job)
        for i, (_, ks, _, out) in enumerate(units):
            o2 = _dot(p_buf[slot][pair_rows(i), 0:kwid], vload(ks))
            o = jnp.where(head0, o2[0:DIL_Q, :], o2[DIL_Q:2 * DIL_Q, :])
            ob[out, :] = o * inv_buf[slot][DIL_Q * i:DIL_Q * (i + 1), :]

    _pipeline3(len(branches) * ngroup, scores, softmax, values)

    for src16, dst in ((ob16, f0), (lb16, f1)):
        for r in range(4):
            dst[pl.ds(r, len4, stride=4), :] = src16[len4 * r:len4 * (r + 1), :]

    def combine(c, carry):
        sl = pl.ds(pl.multiple_of(c * 256, 256), 256)
        l0, l1, l2 = lb1[sl, :], lb4[sl, :], f1[sl, :]
        m = jnp.maximum(jnp.maximum(l0, l1), l2)
        w0, w1, w2 = jnp.exp2(l0 - m), jnp.exp2(l1 - m), jnp.exp2(l2 - m)
        mix = (w0 * ob1[sl, :] + w1 * ob4[sl, :] + w2 * f0[sl, :]) / (w0 + w1 + w2)
        o_ref[0, sl, :] = mix.astype(BF16)
        return carry

    lax.fori_loop(0, seq // 256, combine, 0)


def _dilated(qa, ka, va, biases):
    b, s, _ = qa.shape
    npair = A_HEADS // 2
    tok = pl.BlockSpec((1, s, LANES), lambda hp, bi: (bi, 0, hp))

    def bias_spec(arr):
        n = arr.shape[0] // npair
        return pl.BlockSpec((n,) + arr.shape[1:], lambda hp, bi: (hp, 0, 0))

    return pl.pallas_call(
        functools.partial(_dil_kernel, seq=s),
        grid=(npair, b),
        in_specs=[tok, tok, tok] + [bias_spec(a) for a in biases],
        out_specs=tok,
        out_shape=jax.ShapeDtypeStruct((b, s, A_WIDTH), BF16),
        scratch_shapes=([pltpu.VMEM((s, LANES), F32)] * 6
                        + [pltpu.VMEM((n * s, LANES), BF16) for n in (2, 2, 1, 1, 2, 1, 1)]
                        + [pltpu.VMEM((s, LANES), F32)] * 6
                        + [pltpu.VMEM((2 * DIL_Q * DIL_GROUP, 2 * DIL_Q), F32)] * 2
                        + [pltpu.VMEM((2 * DIL_Q * DIL_GROUP, 2 * DIL_Q), BF16)] * 2
                        + [pltpu.VMEM((DIL_Q * DIL_GROUP, LANES), F32)] * 2),
        compiler_params=_params(("parallel", "arbitrary")),
        name="dilated_attn",
    )(qa, ka, va, *biases)


def _gelu(x):
    c = math.sqrt(2.0 / math.pi)
    return x * (0.5 * (1.0 + jnp.tanh(c * (x + 0.044715 * (x * x * x)))))


def _odd_in_kernel(x_ref, g_ref, w_ref, vg_ref, ws_ref, bs_ref, c_ref, qd_ref, kd_ref, vd_ref):
    h = _rms(x_ref[...], g_ref[...]).astype(BF16)
    base = 2 * C_WIDTH
    v = _gelu(_dot(h, w_ref[:, C_WIDTH:2 * C_WIDTH]))
    u = _gelu(_dot(h, w_ref[:, 0:C_WIDTH]))
    qd_ref[...] = (_dot(h, w_ref[:, base:base + D_WIDTH]) * (HEAD_DIM ** -0.5 * LOG2E)).astype(BF16)
    kd_ref[...] = _dot(h, w_ref[:, base + D_WIDTH:base + 2 * D_WIDTH]).astype(BF16)
    vd_ref[...] = _dot(h, w_ref[:, base + 2 * D_WIDTH:base + 3 * D_WIDTH]).astype(BF16)
    mu = jnp.mean(v, -1, keepdims=True)
    vc = v - mu
    var = jnp.mean(vc * vc, -1, keepdims=True)
    vn = (vc * lax.rsqrt(var + EPS) * vg_ref[...]).astype(BF16)
    lane = lax.broadcasted_iota(jnp.int32, (C_CHUNK, LANES), 1)
    group0 = lane < C_GROUP_W
    for c in range(x_ref.shape[0] // C_CHUNK):
        rs = slice(C_CHUNK * c, C_CHUNK * (c + 1))
        for pair in range(C_GROUPS // 2):
            ls = slice(LANES * pair, LANES * (pair + 1))
            vp = vn[rs, ls]
            sv = jnp.where(group0, _dot(ws_ref[2 * pair], vp), _dot(ws_ref[2 * pair + 1], vp)) + bs_ref[:, ls]
            c_ref[rs, ls] = (u[rs, ls] * sv).astype(BF16)


def _odd_in(x2, g, w_in, vg, ws, bs_tile):
    t, d = x2.shape
    tm = ROW_TILE
    row = lambda n: pl.BlockSpec((tm, n), lambda i: (i, 0))
    out_shapes = [jax.ShapeDtypeStruct((t, n), BF16) for n in (C_WIDTH, D_WIDTH, D_WIDTH, D_WIDTH)]
    return pl.pallas_call(
        _odd_in_kernel,
        grid=(t // tm,),
        in_specs=[row(d), _const_spec(g.shape), _const_spec(w_in.shape), _const_spec(vg.shape),
                  _const_spec(ws.shape), _const_spec(bs_tile.shape)],
        out_specs=[row(s.shape[1]) for s in out_shapes],
        out_shape=out_shapes,
        compiler_params=_params(("parallel",)),
        name="odd_in",
    )(x2, g, w_in, vg, ws, bs_tile)


def _na_geometry(rows):
    kr = min(NA_ROWS, rows)
    nunit = rows // NA_QROWS
    kstart = [int(np.clip(NA_QROWS * j - kr // 2, 0, rows - NA_KROWS)) for j in range(nunit)]
    keys = [(kstart[j] - NA_QROWS * j,
             tuple(int(np.clip(i - kr // 2, 0, rows - kr)) - i for i in range(NA_QROWS * j, NA_QROWS * (j + 1))))
            for j in range(nunit)]
    patterns = sorted(set(keys), key=keys.index)
    return kstart, [patterns.index(k) for k in keys], patterns


def _na_rows(rows):
    _, _, patterns = _na_geometry(rows)
    lo = min(p[0] for p in patterns) - (NA_QROWS - 1) + NA_ROWS - 1
    hi = max(p[0] for p in patterns) + NA_KROWS - 1 + NA_ROWS - 1
    return min(lo, 0), max(hi, 2 * NA_ROWS - 2)


def _na_bias(rpb, rows):
    lo, hi = _na_rows(rows)
    qc = np.arange(GRID_W)[:, None]
    kc = np.arange(GRID_W)[None, :]
    cstart = np.clip(qc - NA_COLS // 2, 0, GRID_W - NA_COLS)
    col_ok = (kc >= cstart) & (kc < cstart + NA_COLS)
    rpb = rpb.astype(F32) * LOG2E
    gap = jnp.zeros(rpb.shape[:-1] + (2 * GRID_W - (2 * NA_COLS - 1),), F32)
    by_col = _toeplitz(jnp.concatenate([rpb[..., NA_COLS - 1:], gap, rpb[..., :NA_COLS - 1]], -1), GRID_W, GRID_W)
    by_col = jnp.where(col_ok[None, None], by_col, NEG_INF)
    by_col = jnp.pad(by_col, ((0, 0), (-lo, hi - (2 * NA_ROWS - 2)), (0, 0), (0, 0)))
    return jnp.concatenate([by_col[:, :-1], by_col[:, 1:]], -1)


def _na_kernel(q_ref, k_ref, v_ref, b_ref, o_ref, qq, s0, s1, p0, p1, i0, i1, *, rows):
    s_buf, p_buf, inv_buf = (s0, s1), (p0, p1), (i0, i1)
    kstart, pattern, patterns = _na_geometry(rows)
    nq = NA_QROWS * GRID_W
    nk = NA_KROWS * GRID_W
    assert 2 * GRID_W == LANES and NA_KROWS % 2 == 0
    head0 = lax.broadcasted_iota(jnp.int32, (nq, LANES), 1) < HEAD_DIM
    for j in range(rows // NA_QROWS):
        t = q_ref[0, nq * j:nq * (j + 1), :].astype(F32)
        qq[2 * nq * j:2 * nq * j + nq, :] = jnp.where(head0, t, 0.0).astype(BF16)
        qq[2 * nq * j + nq:2 * nq * (j + 1), :] = jnp.where(head0, 0.0, t).astype(BF16)

    lo, _ = _na_rows(rows)
    kr = min(NA_ROWS, rows)
    low_half = lax.broadcasted_iota(jnp.int32, (GRID_W, LANES), 1) < GRID_W

    def bias_tile(h, j):
        kstart_rel, r0_rel = patterns[pattern[j]]
        tile_rows = []
        for a in range(NA_QROWS):
            blocks = []
            for ka in range(0, NA_KROWS, 2):
                krel = kstart_rel + ka - a
                ok = [r0_rel[a] <= krel + i < r0_rel[a] + kr for i in range(2)]
                if not any(ok):
                    blocks.append(jnp.full((GRID_W, LANES), NEG_INF, F32))
                    continue
                blk = b_ref[h, krel + NA_ROWS - 1 - lo]
                if not ok[0]:
                    blk = jnp.where(low_half, NEG_INF, blk)
                if not ok[1]:
                    blk = jnp.where(low_half, blk, NEG_INF)
                blocks.append(blk)
            tile_rows.append(jnp.concatenate(blocks, axis=1))
        return jnp.concatenate(tile_rows, axis=0)

    def window(j):
        return slice(nq * j, nq * (j + 1)), slice(GRID_W * kstart[j], GRID_W * kstart[j] + nk)

    def scores(j, slot):
        _, ks = window(j)
        bias = jnp.concatenate([bias_tile(0, j), bias_tile(1, j)], axis=0)
        s_buf[slot][...] = _dot_nt(qq[2 * nq * j:2 * nq * (j + 1), :], k_ref[0, ks, :]) + bias

    def softmax(j, slot):
        s = s_buf[slot][...]
        m = jnp.max(s, -1, keepdims=True)
        p = jnp.exp2(s - m)
        p_buf[slot][...] = p.astype(BF16)
        den = jnp.sum(p, -1, keepdims=True)
        inv_buf[slot][...] = 1.0 / jnp.where(head0, den[0:nq, :], den[nq:2 * nq, :])

    def values(j, slot):
        qs, ks = window(j)
        o2 = _dot(p_buf[slot][...], v_ref[0, ks, :])
        o_ref[0, qs, :] = (jnp.where(head0, o2[0:nq, :], o2[nq:2 * nq, :]) * inv_buf[slot][...]).astype(BF16)

    _pipeline3(rows // NA_QROWS, scores, softmax, values)


def _neighbourhood(qd, kd, vd, bias):
    b, s, _ = qd.shape
    npair = D_HEADS // 2
    tok = pl.BlockSpec((1, s, LANES), lambda hp, bi: (bi, 0, hp))
    return pl.pallas_call(
        functools.partial(_na_kernel, rows=s // GRID_W),
        grid=(npair, b),
        in_specs=[tok, tok, tok, pl.BlockSpec((2,) + bias.shape[1:], lambda hp, bi: (hp, 0, 0, 0))],
        out_specs=tok,
        out_shape=jax.ShapeDtypeStruct((b, s, D_WIDTH), BF16),
        scratch_shapes=([pltpu.VMEM((2 * s, LANES), BF16)]
                        + [pltpu.VMEM((2 * NA_QROWS * GRID_W, NA_KROWS * GRID_W), F32)] * 2
                        + [pltpu.VMEM((2 * NA_QROWS * GRID_W, NA_KROWS * GRID_W), BF16)] * 2
                        + [pltpu.VMEM((NA_QROWS * GRID_W, LANES), F32)] * 2),
        compiler_params=_params(("parallel", "arbitrary")),
        name="neighbourhood_attn",
    )(qd, kd, vd, bias)


def _mix_ffn_kernel(x_ref, a_ref, b_ref, wo_ref, g_ref, wgu_ref, wd_ref, fg_ref, o_ref, *, final, nchunk):
    half = a_ref.shape[1]
    x = x_ref[...] + _dot(a_ref[...], wo_ref[0:half, :]) + _dot(b_ref[...], wo_ref[half:2 * half, :])
    h = _rms(x, g_ref[...]).astype(BF16)
    hidden = wd_ref.shape[0]
    hc = hidden // nchunk
    acc = x
    for c in range(nchunk):
        gate = _dot(h, wgu_ref[:, hc * c:hc * (c + 1)])
        up = _dot(h, wgu_ref[:, hidden + hc * c:hidden + hc * (c + 1)])
        act = (gate * (1.0 / (1.0 + jnp.exp(-gate))) * up).astype(BF16)
        acc = acc + _dot(act, wd_ref[hc * c:hc * (c + 1), :])
    o_ref[...] = _rms(acc, fg_ref[...]) if final else acc


def _mix_ffn(x2, a, b, wo, g, wgu, wd, fg, final):
    t, d = x2.shape
    tm = ROW_TILE
    row = lambda n: pl.BlockSpec((tm, n), lambda i: (i, 0))
    return pl.pallas_call(
        functools.partial(_mix_ffn_kernel, final=final, nchunk=1),
        grid=(t // tm,),
        in_specs=[row(d), row(a.shape[1]), row(b.shape[1]), _const_spec(wo.shape), _const_spec(g.shape),
                  _const_spec(wgu.shape), _const_spec(wd.shape), _const_spec(fg.shape)],
        out_specs=row(d),
        out_shape=jax.ShapeDtypeStruct((t, d), F32),
        compiler_params=_params(("parallel",)),
        name="mix_ffn",
    )(x2, a, b, wo, g, wgu, wd, fg)


def _rope_tables(seq):
    pos = jnp.arange(seq, dtype=F32)
    inv = 1.0 / (ROPE_BASE ** (jnp.arange(0, B_ROPE, 2, dtype=F32) / B_ROPE))
    ang = pos[:, None] * inv[None, :]
    cos, sin = jnp.cos(ang), jnp.sin(ang)
    hf = B_ROPE // 2
    ones = jnp.ones((seq, B_NOPE), F32)
    z = lambda n: jnp.zeros((seq, n), F32)
    tail = LANES - B_NOPE - B_ROPE
    c = jnp.concatenate([ones, cos, cos, z(tail)], 1)
    s1 = jnp.concatenate([z(B_NOPE), -sin, z(hf), z(tail)], 1)
    s2 = jnp.concatenate([z(B_NOPE), z(hf), sin, z(tail)], 1)
    return c, s1, s2


def _pad_heads(w, heads, width):
    k = w.shape[0]
    return jnp.pad(w.reshape(k, heads, width), ((0, 0), (0, 0), (0, LANES - width))).reshape(k, heads * LANES)


def kernel(x, t5_bias, norm_mix, norm_ffn, ev_w_in, ev_q_gain, ev_kv_gain, ev_w_uq, ev_w_ukv, ev_w_out,
           od_w_in, od_v_gain, od_w_s, od_b_s, od_rpb, od_w_out, ffn_w_gu, ffn_w_down, final_gain):
    bsz, seq, d = x.shape
    depth = norm_mix.shape[0]
    t = bsz * seq
    assert seq % ROW_TILE == 0 and seq % GRID_W == 0 and ROW_TILE % C_CHUNK == 0
    x2 = x.reshape(t, d)
    rope_c, rope_s1, rope_s2 = _rope_tables(seq)
    dil_bias = [_dil_bias(t5_bias, seq, dil) for _, dil in A_BRANCHES]
    row = lambda v: v.reshape(1, -1).astype(F32)
    tok3 = lambda a: a.reshape(bsz, seq, a.shape[-1])
    tok2 = lambda a: a.reshape(t, a.shape[-1])

    for layer in range(depth):
        j = layer // 2
        if layer % 2 == 0:
            w_in = ev_w_in[j]
            o2 = 3 * A_WIDTH + B_Q_RANK + B_KV_RANK
            w_in = jnp.concatenate([w_in[:, :o2], jnp.zeros((d, B_NOPE), F32), w_in[:, o2:],
                                    jnp.zeros((d, LANES - B_NOPE - B_ROPE), F32)], 1).astype(BF16)
            wuq = _pad_heads(ev_w_uq[j], B_HEADS, B_NOPE + B_ROPE).astype(BF16)
            wukv = ev_w_ukv[j].reshape(B_KV_RANK, B_HEADS, B_NOPE + B_V)
            wk = _pad_heads(wukv[:, :, :B_NOPE].reshape(B_KV_RANK, -1), B_HEADS, B_NOPE).astype(BF16)
            wv = wukv[:, :, B_NOPE:].reshape(B_KV_RANK, -1).astype(BF16)
            qa, ka, va, q, k, v = _even_in(x2, row(norm_mix[layer]), w_in, row(ev_q_gain[j]), row(ev_kv_gain[j]),
                                           wuq, wk, wv, rope_c, rope_s1, rope_s2, seq)
            m0 = tok2(_dilated(tok3(qa), tok3(ka), tok3(va), dil_bias))
            m1 = tok2(_mla(tok3(q), tok3(k), tok3(v)))
            wo = ev_w_out[j]
        else:
            bs_tile = jnp.repeat(jnp.transpose(od_b_s[j]), C_GROUP_W, axis=1).astype(F32)
            m0, qd, kd, vd = _odd_in(x2, row(norm_mix[layer]), od_w_in[j].astype(BF16), row(od_v_gain[j]),
                                     od_w_s[j].astype(BF16), bs_tile)
            m1 = tok2(_neighbourhood(tok3(qd), tok3(kd), tok3(vd), _na_bias(od_rpb[j], seq // GRID_W)))
            wo = od_w_out[j]
        x2 = _mix_ffn(x2, m0, m1, wo.astype(BF16), row(norm_ffn[layer]), ffn_w_gu[layer].astype(BF16),
                      ffn_w_down[layer].astype(BF16), row(final_gain), final=layer == depth - 1)
    return x2.reshape(bsz, seq, d)
```

```python
import functools
import math

import jax
import jax.numpy as jnp
import numpy as np
from jax import lax
from jax.experimental import pallas as pl
from jax.experimental.pallas import tpu as pltpu

HEAD_DIM = 64
EPS = 1e-6
NEG_INF = -1e30
LOG2E = math.log2(math.e)
A_HEADS = 8
A_BRANCHES = ((128, 1), (512, 4), (2048, 16))
T5_BUCKETS = 32
T5_MAX_DIST = 1024
B_HEADS = 8
B_Q_RANK = 768
B_KV_RANK = 256
B_NOPE = 64
B_ROPE = 32
B_V = 64
ROPE_BASE = 10000.0
C_GROUPS = 8
C_GROUP_W = 64
C_CHUNK = 128
C_WIDTH = C_GROUPS * C_GROUP_W
D_HEADS = 8
GRID_W = 64
NA_ROWS = 8
NA_COLS = 16
A_WIDTH = A_HEADS * HEAD_DIM
D_WIDTH = D_HEADS * HEAD_DIM

LANES = 128
VMEM_LIMIT = 56 * 1024 * 1024
ROW_TILE = 512
MLA_Q_TILE = 512
DIL_Q = 128
DIL_HALF = 64
DIL_GROUP = 1
NA_QROWS = 1
NA_KROWS = 8

BF16 = jnp.bfloat16
F32 = jnp.float32


def _dot(a, b):
    return jnp.dot(a, b, preferred_element_type=F32)


def _dot_nt(a, b):
    return lax.dot_general(a, b, (((1,), (1,)), ((), ())), preferred_element_type=F32)


def _rms(x, g):
    return x * lax.rsqrt(jnp.mean(x * x, -1, keepdims=True) + EPS) * g


def _const_spec(shape):
    nd = len(shape)
    return pl.BlockSpec(shape, lambda *_: (0,) * nd, pipeline_mode=pl.Buffered(1))


def _params(sem):
    return pltpu.CompilerParams(dimension_semantics=sem, vmem_limit_bytes=VMEM_LIMIT)


def _pipeline3(njobs, stage_a, stage_b, stage_c):
    for step in range(njobs + 2):
        for lag, stage in enumerate((stage_a, stage_b, stage_c)):
            j = step - lag
            if 0 <= j < njobs:
                stage(j, j % 2)


def _rope(t, c, s1, s2):
    return t * c + pltpu.roll(t, LANES - B_ROPE // 2, 1) * s1 + pltpu.roll(t, B_ROPE // 2, 1) * s2


def _even_in_kernel(x_ref, g_ref, w_ref, qg_ref, kvg_ref, wuq_ref, wk_ref, wv_ref, c_ref, s1_ref, s2_ref,
                    qa_ref, ka_ref, va_ref, q_ref, k_ref, v_ref):
    h = _rms(x_ref[...], g_ref[...]).astype(BF16)
    aw = A_WIDTH
    o0 = 3 * aw
    o1 = o0 + B_Q_RANK
    o2 = o1 + B_KV_RANK
    cq = _rms(_dot(h, w_ref[:, o0:o1]), qg_ref[...]).astype(BF16)
    ckv = _rms(_dot(h, w_ref[:, o1:o2]), kvg_ref[...]).astype(BF16)
    c, s1, s2 = c_ref[...], s1_ref[...], s2_ref[...]
    kpe = _rope(_dot(h, w_ref[:, o2:o2 + LANES]), c, s1, s2)
    qa_ref[...] = _dot(h, w_ref[:, 0:aw]).astype(BF16)
    q = _dot(cq, wuq_ref[...])
    kn = _dot(ckv, wk_ref[...])
    v_ref[...] = _dot(ckv, wv_ref[...]).astype(BF16)
    ka_ref[...] = _dot(h, w_ref[:, aw:2 * aw]).astype(BF16)
    qscale = (B_NOPE + B_ROPE) ** -0.5 * LOG2E
    for hh in range(B_HEADS):
        sl = slice(LANES * hh, LANES * (hh + 1))
        q_ref[:, sl] = (_rope(q[:, sl], c, s1, s2) * qscale).astype(BF16)
        k_ref[:, sl] = (kn[:, sl] + kpe).astype(BF16)
    va_ref[...] = _dot(h, w_ref[:, 2 * aw:3 * aw]).astype(BF16)


def _even_in(x2, g, w_in, qg, kvg, wuq, wk, wv, rope_c, rope_s1, rope_s2, seq):
    t, d = x2.shape
    tm = ROW_TILE
    nseq = seq // tm
    row = lambda n: pl.BlockSpec((tm, n), lambda i: (i, 0))
    pos = pl.BlockSpec((tm, LANES), lambda i: (i % nseq, 0))
    qk_w = B_HEADS * LANES
    out_shapes = [jax.ShapeDtypeStruct((t, n), BF16) for n in (A_WIDTH, A_WIDTH, A_WIDTH, qk_w, qk_w, B_HEADS * B_V)]
    return pl.pallas_call(
        _even_in_kernel,
        grid=(t // tm,),
        in_specs=[row(d), _const_spec(g.shape), _const_spec(w_in.shape), _const_spec(qg.shape),
                  _const_spec(kvg.shape), _const_spec(wuq.shape), _const_spec(wk.shape), _const_spec(wv.shape),
                  pos, pos, pos],
        out_specs=[row(s.shape[1]) for s in out_shapes],
        out_shape=out_shapes,
        compiler_params=_params(("parallel",)),
        name="even_in",
    )(x2, g, w_in, qg, kvg, wuq, wk, wv, rope_c, rope_s1, rope_s2)


def _mla_kernel(q_ref, k_ref, v_ref, o_ref, s_a, s_b, p_a, p_b, inv_a, inv_b, va, vb, acc):
    seq = q_ref.shape[1]
    sub = MLA_Q_TILE
    head0 = lax.broadcasted_iota(jnp.int32, (sub, LANES), 1) < B_V
    for t in range(seq // sub):
        rs = slice(sub * t, sub * (t + 1))
        x = v_ref[0, rs, :].astype(F32)
        va[rs, :] = jnp.where(head0, x, 0.0).astype(BF16)
        vb[rs, :] = jnp.where(head0, 0.0, x).astype(BF16)
    jobs = [(t, h) for t in range(seq // sub) for h in range(2)]
    s_buf, p_buf, inv_buf = (s_a, s_b), (p_a, p_b), (inv_a, inv_b)

    def scores(j, slot):
        t, h = jobs[j]
        sl = slice(LANES * h, LANES * (h + 1))
        s_buf[slot][...] = _dot_nt(q_ref[0, sub * t:sub * (t + 1), sl], k_ref[0, :, sl])

    def softmax(j, slot):
        s = s_buf[slot][...]
        m = jnp.max(s, -1, keepdims=True)
        p = jnp.exp2(s - m)
        p_buf[slot][...] = p.astype(BF16)
        inv_buf[slot][...] = jnp.broadcast_to(1.0 / jnp.sum(p, -1, keepdims=True), (sub, LANES))

    def values(j, slot):
        t, h = jobs[j]
        o = _dot(p_buf[slot][...], (va, vb)[h][...]) * inv_buf[slot][...]
        if h == 0:
            acc[...] = o
        else:
            o_ref[0, sub * t:sub * (t + 1), :] = (acc[...] + o).astype(BF16)

    _pipeline3(len(jobs), scores, softmax, values)


def _mla(q, k, v):
    b, s, _ = q.shape
    sub = MLA_Q_TILE
    return pl.pallas_call(
        _mla_kernel,
        grid=(b, B_HEADS // 2),
        in_specs=[pl.BlockSpec((1, s, 2 * LANES), lambda bi, hp: (bi, 0, hp)),
                  pl.BlockSpec((1, s, 2 * LANES), lambda bi, hp: (bi, 0, hp)),
                  pl.BlockSpec((1, s, LANES), lambda bi, hp: (bi, 0, hp))],
        out_specs=pl.BlockSpec((1, s, LANES), lambda bi, hp: (bi, 0, hp)),
        out_shape=jax.ShapeDtypeStruct((b, s, B_HEADS * B_V), BF16),
        scratch_shapes=([pltpu.VMEM((sub, s), F32)] * 2 + [pltpu.VMEM((sub, s), BF16)] * 2
                        + [pltpu.VMEM((sub, LANES), F32)] * 2 + [pltpu.VMEM((s, LANES), BF16)] * 2
                        + [pltpu.VMEM((sub, LANES), F32)]),
        compiler_params=_params(("parallel", "parallel")),
        name="mla_attn",
    )(q, k, v)


def _t5_bucket(rel):
    nb = T5_BUCKETS // 2
    max_exact = nb // 2
    n = np.abs(rel)
    large = max_exact + (np.log(np.maximum(n, 1) / max_exact) / math.log(T5_MAX_DIST / max_exact)
                         * (nb - max_exact)).astype(np.int64)
    large = np.minimum(large, nb - 1)
    return ((rel > 0) * nb + np.where(n < max_exact, n, large)).astype(np.int32)


def _dil_geometry(seq, dil):
    length = seq // dil
    assert length % DIL_Q == 0
    if length == DIL_Q:
        return length, DIL_Q, (0,)
    return length, 2 * DIL_Q, (0, DIL_HALF, DIL_Q)


def _toeplitz(vec, nrow, ncol):
    p = vec.shape[-1]
    flat = jnp.tile(vec, (1,) * (vec.ndim - 1) + (nrow,))[..., :nrow * (p - 1)]
    return flat.reshape(vec.shape[:-1] + (nrow, p - 1))[..., :ncol]


def _dil_bias(t5_bias, seq, dil):
    _, kwid, shifts = _dil_geometry(seq, dil)
    period = 2 * (DIL_Q + kwid)
    k = np.arange(period)
    k = np.where(k < kwid, k, k - period)
    off = k[None, :] - np.asarray(shifts)[:, None]
    valid = np.abs(off) <= DIL_HALF
    vec = jnp.transpose(t5_bias[_t5_bucket(off * dil)], (2, 0, 1)).astype(F32)
    vec = jnp.where(valid[None], vec * LOG2E, NEG_INF)
    bias = _toeplitz(vec, DIL_Q, kwid).reshape(A_HEADS // 2, 2, len(shifts), DIL_Q, kwid)
    return jnp.transpose(bias, (0, 2, 1, 3, 4)).reshape(A_HEADS // 2 * len(shifts), 2 * DIL_Q, kwid)


def _dil_kernel(q_ref, k_ref, v_ref, b1_ref, b4_ref, b16_ref, o_ref,
                f0, f1, f2, f3, f4, f5, q1, q4, k4, v4, q16, k16, v16,
                ob1, ob4, ob16, lb1, lb4, lb16, s0, s1, p0, p1, i0, i1, *, seq):
    s_buf, p_buf, inv_buf = (s0, s1), (p0, p1), (i0, i1)
    assert tuple(d for _, d in A_BRANCHES) == (1, 4, 16)
    qscale = HEAD_DIM ** -0.5 * LOG2E
    len4, len16 = seq // 4, seq // 16

    def put_q(dst, row0, t):
        head0 = lax.broadcasted_iota(jnp.int32, t.shape, 1) < HEAD_DIM
        t = t * qscale
        a, b = jnp.where(head0, t, 0.0).astype(BF16), jnp.where(head0, 0.0, t).astype(BF16)
        for i in range(t.shape[0] // DIL_Q):
            us = slice(DIL_Q * i, DIL_Q * (i + 1))
            dst[2 * row0 + 2 * DIL_Q * i:2 * row0 + 2 * DIL_Q * i + DIL_Q, :] = a[us, :]
            dst[2 * row0 + 2 * DIL_Q * i + DIL_Q:2 * row0 + 2 * DIL_Q * (i + 1), :] = b[us, :]

    for c in range(4):
        rs = slice(len4 * c, len4 * (c + 1))
        t = q_ref[0, rs, :].astype(F32)
        f0[rs, :] = t
        put_q(q1, len4 * c, t)
        f1[rs, :] = k_ref[0, rs, :].astype(F32)
        f2[rs, :] = v_ref[0, rs, :].astype(F32)
    for r in range(4):
        rs = slice(len4 * r, len4 * (r + 1))
        t = f0[pl.ds(r, len4, stride=4), :]
        f3[rs, :] = t
        put_q(q4, len4 * r, t)
        t = f1[pl.ds(r, len4, stride=4), :]
        f4[rs, :] = t
        k4[rs, :] = t.astype(BF16)
        t = f2[pl.ds(r, len4, stride=4), :]
        f5[rs, :] = t
        v4[rs, :] = t.astype(BF16)
    for r4 in range(4):
        for rp in range(4):
            rs = slice(len16 * (r4 + 4 * rp), len16 * (r4 + 4 * rp + 1))
            src = pl.ds(len4 * r4 + rp, len16, stride=4)
            put_q(q16, len16 * (r4 + 4 * rp), f3[src, :])
            k16[rs, :] = f4[src, :].astype(BF16)
            v16[rs, :] = f5[src, :].astype(BF16)

    branches = ((1, q1, lambda ks: k_ref[0, ks, :], lambda ks: v_ref[0, ks, :], b1_ref, ob1, lb1),
                (4, q4, lambda ks: k4[ks, :], lambda ks: v4[ks, :], b4_ref, ob4, lb4),
                (16, q16, lambda ks: k16[ks, :], lambda ks: v16[ks, :], b16_ref, ob16, lb16))
    ngroup = seq // (DIL_Q * DIL_GROUP)
    head0 = lax.broadcasted_iota(jnp.int32, (DIL_Q, LANES), 1) < HEAD_DIM

    def geom(job):
        dil = branches[job // ngroup][0]
        length, kwid, shifts = _dil_geometry(seq, dil)
        nqb = length // DIL_Q
        units = []
        for i in range(DIL_GROUP):
            u = (job % ngroup) * DIL_GROUP + i
            seg, qb = divmod(u, nqb)
            kstart = seg * length + min(max(qb * DIL_Q - DIL_HALF, 0), length - kwid)
            case = 0 if len(shifts) == 1 else (0 if qb == 0 else 2 if qb == nqb - 1 else 1)
            if dil == 1:
                out = slice(DIL_Q * u, DIL_Q * (u + 1))
            elif dil == 4:
                out = pl.ds(seg + dil * DIL_Q * qb, DIL_Q, stride=4)
            else:
                out = pl.ds(len4 * (u % 4) + u // 4, DIL_Q, stride=4)
            units.append((u, slice(kstart, kstart + kwid), case, out))
        return kwid, units

    def pair_rows(i):
        return slice(2 * DIL_Q * i, 2 * DIL_Q * (i + 1))

    def scores(job, slot):
        _, q_all, kload, _, b_ref, _, _ = branches[job // ngroup]
        kwid, units = geom(job)
        for i, (u, ks, case, _) in enumerate(units):
            s_buf[slot][pair_rows(i), 0:kwid] = _dot_nt(q_all[pair_rows(u), :], kload(ks)) + b_ref[case]

    def softmax(job, slot):
        lb = branches[job // ngroup][6]
        kwid, units = geom(job)
        s = s_buf[slot][:, 0:kwid]
        m = jnp.max(s, -1, keepdims=True)
        p = jnp.exp2(s - m)
        p_buf[slot][:, 0:kwid] = p.astype(BF16)
        den = jnp.sum(p, -1, keepdims=True)
        for i, (_, _, _, out) in enumerate(units):
            top, bot = slice(2 * DIL_Q * i, 2 * DIL_Q * i + DIL_Q), slice(2 * DIL_Q * i + DIL_Q, 2 * DIL_Q * (i + 1))
            den_c = jnp.where(head0, den[top, :], den[bot, :])
            inv_buf[slot][DIL_Q * i:DIL_Q * (i + 1), :] = 1.0 / den_c
            lb[out, :] = jnp.where(head0, m[top, :], m[bot, :]) + jnp.log2(den_c)

    def values(job, slot):
        _, _, _, vload, _, ob, _ = branches[job // ngroup]
        kwid, units = geom(job)
        for i, (_, ks, _, out) in enumerate(units):
            o2 = _dot(p_buf[slot][pair_rows(i), 0:kwid], vload(ks))
            o = jnp.where(head0, o2[0:DIL_Q, :], o2[DIL_Q:2 * DIL_Q, :])
            ob[out, :] = o * inv_buf[slot][DIL_Q * i:DIL_Q * (i + 1), :]

    _pipeline3(len(branches) * ngroup, scores, softmax, values)

    for src16, dst in ((ob16, f0), (lb16, f1)):
        for r in range(4):
            dst[pl.ds(r, len4, stride=4), :] = src16[len4 * r:len4 * (r + 1), :]

    def combine(c, carry):
        sl = pl.ds(pl.multiple_of(c * 256, 256), 256)
        l0, l1, l2 = lb1[sl, :], lb4[sl, :], f1[sl, :]
        m = jnp.maximum(jnp.maximum(l0, l1), l2)
        w0, w1, w2 = jnp.exp2(l0 - m), jnp.exp2(l1 - m), jnp.exp2(l2 - m)
        mix = (w0 * ob1[sl, :] + w1 * ob4[sl, :] + w2 * f0[sl, :]) / (w0 + w1 + w2)
        o_ref[0, sl, :] = mix.astype(BF16)
        return carry

    lax.fori_loop(0, seq // 256, combine, 0)


def _dilated(qa, ka, va, biases):
    b, s, _ = qa.shape
    npair = A_HEADS // 2
    tok = pl.BlockSpec((1, s, LANES), lambda hp, bi: (bi, 0, hp))

    def bias_spec(arr):
        n = arr.shape[0] // npair
        return pl.BlockSpec((n,) + arr.shape[1:], lambda hp, bi: (hp, 0, 0))

    return pl.pallas_call(
        functools.partial(_dil_kernel, seq=s),
        grid=(npair, b),
        in_specs=[tok, tok, tok] + [bias_spec(a) for a in biases],
        out_specs=tok,
        out_shape=jax.ShapeDtypeStruct((b, s, A_WIDTH), BF16),
        scratch_shapes=([pltpu.VMEM((s, LANES), F32)] * 6
                        + [pltpu.VMEM((n * s, LANES), BF16) for n in (2, 2, 1, 1, 2, 1, 1)]
                        + [pltpu.VMEM((s, LANES), F32)] * 6
                        + [pltpu.VMEM((2 * DIL_Q * DIL_GROUP, 2 * DIL_Q), F32)] * 2
                        + [pltpu.VMEM((2 * DIL_Q * DIL_GROUP, 2 * DIL_Q), BF16)] * 2
                        + [pltpu.VMEM((DIL_Q * DIL_GROUP, LANES), F32)] * 2),
        compiler_params=_params(("parallel", "arbitrary")),
        name="dilated_attn",
    )(qa, ka, va, *biases)


def _gelu(x):
    c = math.sqrt(2.0 / math.pi)
    return x * (0.5 * (1.0 + jnp.tanh(c * (x + 0.044715 * (x * x * x)))))


def _odd_in_kernel(x_ref, g_ref, w_ref, vg_ref, ws_ref, bs_ref, c_ref, qd_ref, kd_ref, vd_ref):
    h = _rms(x_ref[...], g_ref[...]).astype(BF16)
    base = 2 * C_WIDTH
    v = _gelu(_dot(h, w_ref[:, C_WIDTH:2 * C_WIDTH]))
    u = _gelu(_dot(h, w_ref[:, 0:C_WIDTH]))
    qd_ref[...] = (_dot(h, w_ref[:, base:base + D_WIDTH]) * (HEAD_DIM ** -0.5 * LOG2E)).astype(BF16)
    kd_ref[...] = _dot(h, w_ref[:, base + D_WIDTH:base + 2 * D_WIDTH]).astype(BF16)
    vd_ref[...] = _dot(h, w_ref[:, base + 2 * D_WIDTH:base + 3 * D_WIDTH]).astype(BF16)
    mu = jnp.mean(v, -1, keepdims=True)
    vc = v - mu
    var = jnp.mean(vc * vc, -1, keepdims=True)
    vn = (vc * lax.rsqrt(var + EPS) * vg_ref[...]).astype(BF16)
    lane = lax.broadcasted_iota(jnp.int32, (C_CHUNK, LANES), 1)
    group0 = lane < C_GROUP_W
    for c in range(x_ref.shape[0] // C_CHUNK):
        rs = slice(C_CHUNK * c, C_CHUNK * (c + 1))
        for pair in range(C_GROUPS // 2):
            ls = slice(LANES * pair, LANES * (pair + 1))
            vp = vn[rs, ls]
            sv = jnp.where(group0, _dot(ws_ref[2 * pair], vp), _dot(ws_ref[2 * pair + 1], vp)) + bs_ref[:, ls]
            c_ref[rs, ls] = (u[rs, ls] * sv).astype(BF16)


def _odd_in(x2, g, w_in, vg, ws, bs_tile):
    t, d = x2.shape
    tm = ROW_TILE
    row = lambda n: pl.BlockSpec((tm, n), lambda i: (i, 0))
    out_shapes = [jax.ShapeDtypeStruct((t, n), BF16) for n in (C_WIDTH, D_WIDTH, D_WIDTH, D_WIDTH)]
    return pl.pallas_call(
        _odd_in_kernel,
        grid=(t // tm,),
        in_specs=[row(d), _const_spec(g.shape), _const_spec(w_in.shape), _const_spec(vg.shape),
                  _const_spec(ws.shape), _const_spec(bs_tile.shape)],
        out_specs=[row(s.shape[1]) for s in out_shapes],
        out_shape=out_shapes,
        compiler_params=_params(("parallel",)),
        name="odd_in",
    )(x2, g, w_in, vg, ws, bs_tile)


def _na_geometry(rows):
    kr = min(NA_ROWS, rows)
    nunit = rows // NA_QROWS
    kstart = [int(np.clip(NA_QROWS * j - kr // 2, 0, rows - NA_KROWS)) for j in range(nunit)]
    keys = [(kstart[j] - NA_QROWS * j,
             tuple(int(np.clip(i - kr // 2, 0, rows - kr)) - i for i in range(NA_QROWS * j, NA_QROWS * (j + 1))))
            for j in range(nunit)]
    patterns = sorted(set(keys), key=keys.index)
    return kstart, [patterns.index(k) for k in keys], patterns


def _na_rows(rows):
    _, _, patterns = _na_geometry(rows)
    lo = min(p[0] for p in patterns) - (NA_QROWS - 1) + NA_ROWS - 1
    hi = max(p[0] for p in patterns) + NA_KROWS - 1 + NA_ROWS - 1
    return min(lo, 0), max(hi, 2 * NA_ROWS - 2)


def _na_bias(rpb, rows):
    lo, hi = _na_rows(rows)
    qc = np.arange(GRID_W)[:, None]
    kc = np.arange(GRID_W)[None, :]
    cstart = np.clip(qc - NA_COLS // 2, 0, GRID_W - NA_COLS)
    col_ok = (kc >= cstart) & (kc < cstart + NA_COLS)
    rpb = rpb.astype(F32) * LOG2E
    gap = jnp.zeros(rpb.shape[:-1] + (2 * GRID_W - (2 * NA_COLS - 1),), F32)
    by_col = _toeplitz(jnp.concatenate([rpb[..., NA_COLS - 1:], gap, rpb[..., :NA_COLS - 1]], -1), GRID_W, GRID_W)
    by_col = jnp.where(col_ok[None, None], by_col, NEG_INF)
    by_col = jnp.pad(by_col, ((0, 0), (-lo, hi - (2 * NA_ROWS - 2)), (0, 0), (0, 0)))
    return jnp.concatenate([by_col[:, :-1], by_col[:, 1:]], -1)


def _na_kernel(q_ref, k_ref, v_ref, b_ref, o_ref, qq, s0, s1, p0, p1, i0, i1, *, rows):
    s_buf, p_buf, inv_buf = (s0, s1), (p0, p1), (i0, i1)
    kstart, pattern, patterns = _na_geometry(rows)
    nq = NA_QROWS * GRID_W
    nk = NA_KROWS * GRID_W
    assert 2 * GRID_W == LANES and NA_KROWS % 2 == 0
    head0 = lax.broadcasted_iota(jnp.int32, (nq, LANES), 1) < HEAD_DIM
    for j in range(rows // NA_QROWS):
        t = q_ref[0, nq * j:nq * (j + 1), :].astype(F32)
        qq[2 * nq * j:2 * nq * j + nq, :] = jnp.where(head0, t, 0.0).astype(BF16)
        qq[2 * nq * j + nq:2 * nq * (j + 1), :] = jnp.where(head0, 0.0, t).astype(BF16)

    lo, _ = _na_rows(rows)
    kr = min(NA_ROWS, rows)
    low_half = lax.broadcasted_iota(jnp.int32, (GRID_W, LANES), 1) < GRID_W

    def bias_tile(h, j):
        kstart_rel, r0_rel = patterns[pattern[j]]
        tile_rows = []
        for a in range(NA_QROWS):
            blocks = []
            for ka in range(0, NA_KROWS, 2):
                krel = kstart_rel + ka - a
                ok = [r0_rel[a] <= krel + i < r0_rel[a] + kr for i in range(2)]
                if not any(ok):
                    blocks.append(jnp.full((GRID_W, LANES), NEG_INF, F32))
                    continue
                blk = b_ref[h, krel + NA_ROWS - 1 - lo]
                if not ok[0]:
                    blk = jnp.where(low_half, NEG_INF, blk)
                if not ok[1]:
                    blk = jnp.where(low_half, blk, NEG_INF)
                blocks.append(blk)
            tile_rows.append(jnp.concatenate(blocks, axis=1))
        return jnp.concatenate(tile_rows, axis=0)

    def window(j):
        return slice(nq * j, nq * (j + 1)), slice(GRID_W * kstart[j], GRID_W * kstart[j] + nk)

    def scores(j, slot):
        _, ks = window(j)
        bias = jnp.concatenate([bias_tile(0, j), bias_tile(1, j)], axis=0)
        s_buf[slot][...] = _dot_nt(qq[2 * nq * j:2 * nq * (j + 1), :], k_ref[0, ks, :]) + bias

    def softmax(j, slot):
        s = s_buf[slot][...]
        m = jnp.max(s, -1, keepdims=True)
        p = jnp.exp2(s - m)
        p_buf[slot][...] = p.astype(BF16)
        den = jnp.sum(p, -1, keepdims=True)
        inv_buf[slot][...] = 1.0 / jnp.where(head0, den[0:nq, :], den[nq:2 * nq, :])

    def values(j, slot):
        qs, ks = window(j)
        o2 = _dot(p_buf[slot][...], v_ref[0, ks, :])
        o_ref[0, qs, :] = (jnp.where(head0, o2[0:nq, :], o2[nq:2 * nq, :]) * inv_buf[slot][...]).astype(BF16)

    _pipeline3(rows // NA_QROWS, scores, softmax, values)


def _neighbourhood(qd, kd, vd, bias):
    b, s, _ = qd.shape
    npair = D_HEADS // 2
    tok = pl.BlockSpec((1, s, LANES), lambda hp, bi: (bi, 0, hp))
    return pl.pallas_call(
        functools.partial(_na_kernel, rows=s // GRID_W),
        grid=(npair, b),
        in_specs=[tok, tok, tok, pl.BlockSpec((2,) + bias.shape[1:], lambda hp, bi: (hp, 0, 0, 0))],
        out_specs=tok,
        out_shape=jax.ShapeDtypeStruct((b, s, D_WIDTH), BF16),
        scratch_shapes=([pltpu.VMEM((2 * s, LANES), BF16)]
                        + [pltpu.VMEM((2 * NA_QROWS * GRID_W, NA_KROWS * GRID_W), F32)] * 2
                        + [pltpu.VMEM((2 * NA_QROWS * GRID_W, NA_KROWS * GRID_W), BF16)] * 2
                        + [pltpu.VMEM((NA_QROWS * GRID_W, LANES), F32)] * 2),
        compiler_params=_params(("parallel", "arbitrary")),
        name="neighbourhood_attn",
    )(qd, kd, vd, bias)


def _mix_ffn_kernel(x_ref, a_ref, b_ref, wo_ref, g_ref, wgu_ref, wd_ref, fg_ref, o_ref, *, final, nchunk):
    half = a_ref.shape[1]
    x = x_ref[...] + _dot(a_ref[...], wo_ref[0:half, :]) + _dot(b_ref[...], wo_ref[half:2 * half, :])
    h = _rms(x, g_ref[...]).astype(BF16)
    hidden = wd_ref.shape[0]
    hc = hidden // nchunk
    acc = x
    for c in range(nchunk):
        gate = _dot(h, wgu_ref[:, hc * c:hc * (c + 1)])
        up = _dot(h, wgu_ref[:, hidden + hc * c:hidden + hc * (c + 1)])
        act = (gate * (1.0 / (1.0 + jnp.exp(-gate))) * up).astype(BF16)
        acc = acc + _dot(act, wd_ref[hc * c:hc * (c + 1), :])
    o_ref[...] = _rms(acc, fg_ref[...]) if final else acc


def _mix_ffn(x2, a, b, wo, g, wgu, wd, fg, final):
    t, d = x2.shape
    tm = ROW_TILE
    row = lambda n: pl.BlockSpec((tm, n), lambda i: (i, 0))
    return pl.pallas_call(
        functools.partial(_mix_ffn_kernel, final=final, nchunk=1),
        grid=(t // tm,),
        in_specs=[row(d), row(a.shape[1]), row(b.shape[1]), _const_spec(wo.shape), _const_spec(g.shape),
                  _const_spec(wgu.shape), _const_spec(wd.shape), _const_spec(fg.shape)],
        out_specs=row(d),
        out_shape=jax.ShapeDtypeStruct((t, d), F32),
        compiler_params=_params(("parallel",)),
        name="mix_ffn",
    )(x2, a, b, wo, g, wgu, wd, fg)


def _rope_tables(seq):
    pos = jnp.arange(seq, dtype=F32)
    inv = 1.0 / (ROPE_BASE ** (jnp.arange(0, B_ROPE, 2, dtype=F32) / B_ROPE))
    ang = pos[:, None] * inv[None, :]
    cos, sin = jnp.cos(ang), jnp.sin(ang)
    hf = B_ROPE // 2
    ones = jnp.ones((seq, B_NOPE), F32)
    z = lambda n: jnp.zeros((seq, n), F32)
    tail = LANES - B_NOPE - B_ROPE
    c = jnp.concatenate([ones, cos, cos, z(tail)], 1)
    s1 = jnp.concatenate([z(B_NOPE), -sin, z(hf), z(tail)], 1)
    s2 = jnp.concatenate([z(B_NOPE), z(hf), sin, z(tail)], 1)
    return c, s1, s2


def _pad_heads(w, heads, width):
    k = w.shape[0]
    return jnp.pad(w.reshape(k, heads, width), ((0, 0), (0, 0), (0, LANES - width))).reshape(k, heads * LANES)


def kernel(x, t5_bias, norm_mix, norm_ffn, ev_w_in, ev_q_gain, ev_kv_gain, ev_w_uq, ev_w_ukv, ev_w_out,
           od_w_in, od_v_gain, od_w_s, od_b_s, od_rpb, od_w_out, ffn_w_gu, ffn_w_down, final_gain):
    bsz, seq, d = x.shape
    depth = norm_mix.shape[0]
    t = bsz * seq
    assert seq % ROW_TILE == 0 and seq % GRID_W == 0 and ROW_TILE % C_CHUNK == 0
    x2 = x.reshape(t, d)
    rope_c, rope_s1, rope_s2 = _rope_tables(seq)
    dil_bias = [_dil_bias(t5_bias, seq, dil) for _, dil in A_BRANCHES]
    row = lambda v: v.reshape(1, -1).astype(F32)
    tok3 = lambda a: a.reshape(bsz, seq, a.shape[-1])
    tok2 = lambda a: a.reshape(t, a.shape[-1])

    for layer in range(depth):
        j = layer // 2
        if layer % 2 == 0:
            w_in = ev_w_in[j]
            o2 = 3 * A_WIDTH + B_Q_RANK + B_KV_RANK
            w_in = jnp.concatenate([w_in[:, :o2], jnp.zeros((d, B_NOPE), F32), w_in[:, o2:],
                                    jnp.zeros((d, LANES - B_NOPE - B_ROPE), F32)], 1).astype(BF16)
            wuq = _pad_heads(ev_w_uq[j], B_HEADS, B_NOPE + B_ROPE).astype(BF16)
            wukv = ev_w_ukv[j].reshape(B_KV_RANK, B_HEADS, B_NOPE + B_V)
            wk = _pad_heads(wukv[:, :, :B_NOPE].reshape(B_KV_RANK, -1), B_HEADS, B_NOPE).astype(BF16)
            wv = wukv[:, :, B_NOPE:].reshape(B_KV_RANK, -1).astype(BF16)
            qa, ka, va, q, k, v = _even_in(x2, row(norm_mix[layer]), w_in, row(ev_q_gain[j]), row(ev_kv_gain[j]),
                                           wuq, wk, wv, rope_c, rope_s1, rope_s2, seq)
            m0 = tok2(_dilated(tok3(qa), tok3(ka), tok3(va), dil_bias))
            m1 = tok2(_mla(tok3(q), tok3(k), tok3(v)))
            wo = ev_w_out[j]
        else:
            bs_tile = jnp.repeat(jnp.transpose(od_b_s[j]), C_GROUP_W, axis=1).astype(F32)
            m0, qd, kd, vd = _odd_in(x2, row(norm_mix[layer]), od_w_in[j].astype(BF16), row(od_v_gain[j]),
                                     od_w_s[j].astype(BF16), bs_tile)
            m1 = tok2(_neighbourhood(tok3(qd), tok3(kd), tok3(vd), _na_bias(od_rpb[j], seq // GRID_W)))
            wo = od_w_out[j]
        x2 = _mix_ffn(x2, m0, m1, wo.astype(BF16), row(norm_ffn[layer]), ffn_w_gu[layer].astype(BF16),
                      ffn_w_down[layer].astype(BF16), row(final_gain), final=layer == depth - 1)
    return x2.reshape(bsz, seq, d)
```

```python
import functools
import math

import jax
import jax.numpy as jnp
import numpy as np
from jax import lax
from jax.experimental import pallas as pl
from jax.experimental.pallas import tpu as pltpu

HEAD_DIM = 64
EPS = 1e-6
NEG_INF = -1e30
LOG2E = math.log2(math.e)
A_HEADS = 8
A_BRANCHES = ((128, 1), (512, 4), (2048, 16))
T5_BUCKETS = 32
T5_MAX_DIST = 1024
B_HEADS = 8
B_Q_RANK = 768
B_KV_RANK = 256
B_NOPE = 64
B_ROPE = 32
B_V = 64
ROPE_BASE = 10000.0
C_GROUPS = 8
C_GROUP_W = 64
C_CHUNK = 128
C_WIDTH = C_GROUPS * C_GROUP_W
D_HEADS = 8
GRID_W = 64
NA_ROWS = 8
NA_COLS = 16
A_WIDTH = A_HEADS * HEAD_DIM
D_WIDTH = D_HEADS * HEAD_DIM

LANES = 128
VMEM_LIMIT = 56 * 1024 * 1024
ROW_TILE = 512
IN_TILE = 1024
MLA_Q_TILE = 256
DIL_Q = 128
DIL_HALF = 64
DIL_GROUP = 1
NA_QROWS = 1
NA_KROWS = 8

BF16 = jnp.bfloat16
F32 = jnp.float32


def _dot(a, b):
    return jnp.dot(a, b, preferred_element_type=F32)


def _dot_nt(a, b):
    return lax.dot_general(a, b, (((1,), (1,)), ((), ())), preferred_element_type=F32)


def _rms(x, g):
    return x * lax.rsqrt(jnp.mean(x * x, -1, keepdims=True) + EPS) * g


def _const_spec(shape):
    nd = len(shape)
    return pl.BlockSpec(shape, lambda *_: (0,) * nd, pipeline_mode=pl.Buffered(1))


def _params(sem):
    return pltpu.CompilerParams(dimension_semantics=sem, vmem_limit_bytes=VMEM_LIMIT)


def _pipeline3(njobs, stage_a, stage_b, stage_c):
    for step in range(njobs + 2):
        for lag, stage in enumerate((stage_a, stage_b, stage_c)):
            j = step - lag
            if 0 <= j < njobs:
                stage(j, j % 2)


def _rope(t, c, s1, s2):
    return t * c + pltpu.roll(t, LANES - B_ROPE // 2, 1) * s1 + pltpu.roll(t, B_ROPE // 2, 1) * s2


def _even_in_kernel(x_ref, g_ref, w_ref, qg_ref, kvg_ref, wuq_ref, wk_ref, wv_ref, c_ref, s1_ref, s2_ref,
                    qa_ref, ka_ref, va_ref, q_ref, k_ref, v_ref):
    h = _rms(x_ref[...], g_ref[...]).astype(BF16)
    aw = A_WIDTH
    o0 = 3 * aw
    o1 = o0 + B_Q_RANK
    o2 = o1 + B_KV_RANK
    cq = _rms(_dot(h, w_ref[:, o0:o1]), qg_ref[...]).astype(BF16)
    ckv = _rms(_dot(h, w_ref[:, o1:o2]), kvg_ref[...]).astype(BF16)
    c, s1, s2 = c_ref[...], s1_ref[...], s2_ref[...]
    kpe = _rope(_dot(h, w_ref[:, o2:o2 + LANES]), c, s1, s2)
    qa_ref[...] = _dot(h, w_ref[:, 0:aw]).astype(BF16)
    q = _dot(cq, wuq_ref[...])
    kn = _dot(ckv, wk_ref[...])
    v_ref[...] = _dot(ckv, wv_ref[...]).astype(BF16)
    ka_ref[...] = _dot(h, w_ref[:, aw:2 * aw]).astype(BF16)
    qscale = (B_NOPE + B_ROPE) ** -0.5 * LOG2E
    for hh in range(B_HEADS):
        sl = slice(LANES * hh, LANES * (hh + 1))
        q_ref[:, sl] = (_rope(q[:, sl], c, s1, s2) * qscale).astype(BF16)
        k_ref[:, sl] = (kn[:, sl] + kpe).astype(BF16)
    va_ref[...] = _dot(h, w_ref[:, 2 * aw:3 * aw]).astype(BF16)


def _even_in(x2, g, w_in, qg, kvg, wuq, wk, wv, rope_c, rope_s1, rope_s2, seq):
    t, d = x2.shape
    tm = IN_TILE
    nseq = seq // tm
    row = lambda n: pl.BlockSpec((tm, n), lambda i: (i, 0))
    pos = pl.BlockSpec((tm, LANES), lambda i: (i % nseq, 0))
    qk_w = B_HEADS * LANES
    out_shapes = [jax.ShapeDtypeStruct((t, n), BF16) for n in (A_WIDTH, A_WIDTH, A_WIDTH, qk_w, qk_w, B_HEADS * B_V)]
    return pl.pallas_call(
        _even_in_kernel,
        grid=(t // tm,),
        in_specs=[row(d), _const_spec(g.shape), _const_spec(w_in.shape), _const_spec(qg.shape),
                  _const_spec(kvg.shape), _const_spec(wuq.shape), _const_spec(wk.shape), _const_spec(wv.shape),
                  pos, pos, pos],
        out_specs=[row(s.shape[1]) for s in out_shapes],
        out_shape=out_shapes,
        compiler_params=_params(("parallel",)),
        name="even_in",
    )(x2, g, w_in, qg, kvg, wuq, wk, wv, rope_c, rope_s1, rope_s2)


def _mla_kernel(q_ref, k_ref, v_ref, o_ref, s_a, s_b, p_a, p_b, inv_a, inv_b):
    seq = q_ref.shape[1]
    sub = MLA_Q_TILE
    head0 = lax.broadcasted_iota(jnp.int32, (sub, LANES), 1) < B_V
    s_buf, p_buf, inv_buf = (s_a, s_b), (p_a, p_b), (inv_a, inv_b)

    def scores(t, slot):
        for h in range(2):
            sl = slice(LANES * h, LANES * (h + 1))
            s_buf[slot][sub * h:sub * (h + 1), :] = _dot_nt(q_ref[0, sub * t:sub * (t + 1), sl], k_ref[0, :, sl])

    def softmax(t, slot):
        s = s_buf[slot][...]
        m = jnp.max(s, -1, keepdims=True)
        p = jnp.exp2(s - m)
        p_buf[slot][...] = p.astype(BF16)
        den = jnp.sum(p, -1, keepdims=True)
        inv_buf[slot][...] = 1.0 / jnp.where(head0, den[0:sub, :], den[sub:2 * sub, :])

    def values(t, slot):
        o2 = _dot(p_buf[slot][...], v_ref[0])
        o = jnp.where(head0, o2[0:sub, :], o2[sub:2 * sub, :]) * inv_buf[slot][...]
        o_ref[0, sub * t:sub * (t + 1), :] = o.astype(BF16)

    _pipeline3(seq // sub, scores, softmax, values)


def _mla(q, k, v):
    b, s, _ = q.shape
    sub = MLA_Q_TILE
    return pl.pallas_call(
        _mla_kernel,
        grid=(b, B_HEADS // 2),
        in_specs=[pl.BlockSpec((1, s, 2 * LANES), lambda bi, hp: (bi, 0, hp)),
                  pl.BlockSpec((1, s, 2 * LANES), lambda bi, hp: (bi, 0, hp)),
                  pl.BlockSpec((1, s, LANES), lambda bi, hp: (bi, 0, hp))],
        out_specs=pl.BlockSpec((1, s, LANES), lambda bi, hp: (bi, 0, hp)),
        out_shape=jax.ShapeDtypeStruct((b, s, B_HEADS * B_V), BF16),
        scratch_shapes=([pltpu.VMEM((2 * sub, s), F32)] * 2 + [pltpu.VMEM((2 * sub, s), BF16)] * 2
                        + [pltpu.VMEM((sub, LANES), F32)] * 2),
        compiler_params=_params(("parallel", "parallel")),
        name="mla_attn",
    )(q, k, v)


def _t5_bucket(rel):
    nb = T5_BUCKETS // 2
    max_exact = nb // 2
    n = np.abs(rel)
    large = max_exact + (np.log(np.maximum(n, 1) / max_exact) / math.log(T5_MAX_DIST / max_exact)
                         * (nb - max_exact)).astype(np.int64)
    large = np.minimum(large, nb - 1)
    return ((rel > 0) * nb + np.where(n < max_exact, n, large)).astype(np.int32)


def _dil_geometry(seq, dil):
    length = seq // dil
    assert length % DIL_Q == 0
    if length == DIL_Q:
        return length, DIL_Q, (0,)
    return length, 2 * DIL_Q, (0, DIL_HALF, DIL_Q)


def _toeplitz(vec, nrow, ncol):
    p = vec.shape[-1]
    flat = jnp.tile(vec, (1,) * (vec.ndim - 1) + (nrow,))[..., :nrow * (p - 1)]
    return flat.reshape(vec.shape[:-1] + (nrow, p - 1))[..., :ncol]


def _dil_bias(t5_bias, seq, dil):
    _, kwid, shifts = _dil_geometry(seq, dil)
    period = 2 * (DIL_Q + kwid)
    k = np.arange(period)
    k = np.where(k < kwid, k, k - period)
    off = k[None, :] - np.asarray(shifts)[:, None]
    valid = np.abs(off) <= DIL_HALF
    vec = jnp.transpose(t5_bias[_t5_bucket(off * dil)], (2, 0, 1)).astype(F32)
    vec = jnp.where(valid[None], vec * LOG2E, NEG_INF)
    bias = _toeplitz(vec, DIL_Q, kwid).reshape(A_HEADS // 2, 2, len(shifts), DIL_Q, kwid)
    return jnp.transpose(bias, (0, 2, 1, 3, 4)).reshape(A_HEADS // 2 * len(shifts), 2 * DIL_Q, kwid)


def _dil_kernel(q_ref, k_ref, v_ref, b1_ref, b4_ref, b16_ref, o_ref,
                f0, f1, f2, f3, f4, f5, q1, q4, k4, v4, q16, k16, v16,
                ob1, ob4, ob16, lb1, lb4, lb16, s0, s1, p0, p1, i0, i1, *, seq):
    s_buf, p_buf, inv_buf = (s0, s1), (p0, p1), (i0, i1)
    assert tuple(d for _, d in A_BRANCHES) == (1, 4, 16)
    qscale = HEAD_DIM ** -0.5 * LOG2E
    len4, len16 = seq // 4, seq // 16

    def put_q(dst, row0, t):
        head0 = lax.broadcasted_iota(jnp.int32, t.shape, 1) < HEAD_DIM
        t = t * qscale
        a, b = jnp.where(head0, t, 0.0).astype(BF16), jnp.where(head0, 0.0, t).astype(BF16)
        for i in range(t.shape[0] // DIL_Q):
            us = slice(DIL_Q * i, DIL_Q * (i + 1))
            dst[2 * row0 + 2 * DIL_Q * i:2 * row0 + 2 * DIL_Q * i + DIL_Q, :] = a[us, :]
            dst[2 * row0 + 2 * DIL_Q * i + DIL_Q:2 * row0 + 2 * DIL_Q * (i + 1), :] = b[us, :]

    for c in range(4):
        rs = slice(len4 * c, len4 * (c + 1))
        t = q_ref[0, rs, :].astype(F32)
        f0[rs, :] = t
        put_q(q1, len4 * c, t)
        f1[rs, :] = k_ref[0, rs, :].astype(F32)
        f2[rs, :] = v_ref[0, rs, :].astype(F32)
    for r in range(4):
        rs = slice(len4 * r, len4 * (r + 1))
        t = f0[pl.ds(r, len4, stride=4), :]
        f3[rs, :] = t
        put_q(q4, len4 * r, t)
        t = f1[pl.ds(r, len4, stride=4), :]
        f4[rs, :] = t
        k4[rs, :] = t.astype(BF16)
        t = f2[pl.ds(r, len4, stride=4), :]
        f5[rs, :] = t
        v4[rs, :] = t.astype(BF16)
    for r4 in range(4):
        for rp in range(4):
            rs = slice(len16 * (r4 + 4 * rp), len16 * (r4 + 4 * rp + 1))
            src = pl.ds(len4 * r4 + rp, len16, stride=4)
            put_q(q16, len16 * (r4 + 4 * rp), f3[src, :])
            k16[rs, :] = f4[src, :].astype(BF16)
            v16[rs, :] = f5[src, :].astype(BF16)

    branches = ((1, q1, lambda ks: k_ref[0, ks, :], lambda ks: v_ref[0, ks, :], b1_ref, ob1, lb1),
                (4, q4, lambda ks: k4[ks, :], lambda ks: v4[ks, :], b4_ref, ob4, lb4),
                (16, q16, lambda ks: k16[ks, :], lambda ks: v16[ks, :], b16_ref, ob16, lb16))
    ngroup = seq // (DIL_Q * DIL_GROUP)
    head0 = lax.broadcasted_iota(jnp.int32, (DIL_Q, LANES), 1) < HEAD_DIM

    def geom(job):
        dil = branches[job // ngroup][0]
        length, kwid, shifts = _dil_geometry(seq, dil)
        nqb = length // DIL_Q
        units = []
        for i in range(DIL_GROUP):
            u = (job % ngroup) * DIL_GROUP + i
            seg, qb = divmod(u, nqb)
            kstart = seg * length + min(max(qb * DIL_Q - DIL_HALF, 0), length - kwid)
            case = 0 if len(shifts) == 1 else (0 if qb == 0 else 2 if qb == nqb - 1 else 1)
            if dil == 1:
                out = slice(DIL_Q * u, DIL_Q * (u + 1))
            elif dil == 4:
                out = pl.ds(seg + dil * DIL_Q * qb, DIL_Q, stride=4)
            else:
                out = pl.ds(len4 * (u % 4) + u // 4, DIL_Q, stride=4)
            units.append((u, slice(kstart, kstart + kwid), case, out))
        return kwid, units

    def pair_rows(i):
        return slice(2 * DIL_Q * i, 2 * DIL_Q * (i + 1))

    def scores(job, slot):
        _, q_all, kload, _, b_ref, _, _ = branches[job // ngroup]
        kwid, units = geom(job)
        for i, (u, ks, case, _) in enumerate(units):
            s_buf[slot][pair_rows(i), 0:kwid] = _dot_nt(q_all[pair_rows(u), :], kload(ks)) + b_ref[case]

    def softmax(job, slot):
        lb = branches[job // ngroup][6]
        kwid, units = geom(job)
        s = s_buf[slot][:, 0:kwid]
        m = jnp.max(s, -1, keepdims=True)
        p = jnp.exp2(s - m)
        p_buf[slot][:, 0:kwid] = p.astype(BF16)
        den = jnp.sum(p, -1, keepdims=True)
        for i, (_, _, _, out) in enumerate(units):
            top, bot = slice(2 * DIL_Q * i, 2 * DIL_Q * i + DIL_Q), slice(2 * DIL_Q * i + DIL_Q, 2 * DIL_Q * (i + 1))
            den_c = jnp.where(head0, den[top, :], den[bot, :])
            inv_buf[slot][DIL_Q * i:DIL_Q * (i + 1), :] = 1.0 / den_c
            lb[out, :] = jnp.where(head0, m[top, :], m[bot, :]) + jnp.log2(den_c)

    def values(job, slot):
        _, _, _, vload, _, ob, _ = branches[job // ngroup]
        kwid, units = geom(job)
        for i, (_, ks, _, out) in enumerate(units):
            o2 = _dot(p_buf[slot][pair_rows(i), 0:kwid], vload(ks))
            o = jnp.where(head0, o2[0:DIL_Q, :], o2[DIL_Q:2 * DIL_Q, :])
            ob[out, :] = o * inv_buf[slot][DIL_Q * i:DIL_Q * (i + 1), :]

    _pipeline3(len(branches) * ngroup, scores, softmax, values)

    for src16, dst in ((ob16, f0), (lb16, f1)):
        for r in range(4):
            dst[pl.ds(r, len4, stride=4), :] = src16[len4 * r:len4 * (r + 1), :]

    def combine(c, carry):
        sl = pl.ds(pl.multiple_of(c * 256, 256), 256)
        l0, l1, l2 = lb1[sl, :], lb4[sl, :], f1[sl, :]
        m = jnp.maximum(jnp.maximum(l0, l1), l2)
        w0, w1, w2 = jnp.exp2(l0 - m), jnp.exp2(l1 - m), jnp.exp2(l2 - m)
        mix = (w0 * ob1[sl, :] + w1 * ob4[sl, :] + w2 * f0[sl, :]) / (w0 + w1 + w2)
        o_ref[0, sl, :] = mix.astype(BF16)
        return carry

    lax.fori_loop(0, seq // 256, combine, 0)


def _dilated(qa, ka, va, biases):
    b, s, _ = qa.shape
    npair = A_HEADS // 2
    tok = pl.BlockSpec((1, s, LANES), lambda hp, bi: (bi, 0, hp))

    def bias_spec(arr):
        n = arr.shape[0] // npair
        return pl.BlockSpec((n,) + arr.shape[1:], lambda hp, bi: (hp, 0, 0))

    return pl.pallas_call(
        functools.partial(_dil_kernel, seq=s),
        grid=(npair, b),
        in_specs=[tok, tok, tok] + [bias_spec(a) for a in biases],
        out_specs=tok,
        out_shape=jax.ShapeDtypeStruct((b, s, A_WIDTH), BF16),
        scratch_shapes=([pltpu.VMEM((s, LANES), F32)] * 6
                        + [pltpu.VMEM((n * s, LANES), BF16) for n in (2, 2, 1, 1, 2, 1, 1)]
                        + [pltpu.VMEM((s, LANES), F32)] * 6
                        + [pltpu.VMEM((2 * DIL_Q * DIL_GROUP, 2 * DIL_Q), F32)] * 2
                        + [pltpu.VMEM((2 * DIL_Q * DIL_GROUP, 2 * DIL_Q), BF16)] * 2
                        + [pltpu.VMEM((DIL_Q * DIL_GROUP, LANES), F32)] * 2),
        compiler_params=_params(("parallel", "arbitrary")),
        name="dilated_attn",
    )(qa, ka, va, *biases)


def _gelu(x):
    c = math.sqrt(2.0 / math.pi)
    return x * (0.5 * (1.0 + jnp.tanh(c * (x + 0.044715 * (x * x * x)))))


def _odd_in_kernel(x_ref, g_ref, w_ref, vg_ref, ws_ref, bs_ref, c_ref, qd_ref, kd_ref, vd_ref):
    h = _rms(x_ref[...], g_ref[...]).astype(BF16)
    base = 2 * C_WIDTH
    v = _gelu(_dot(h, w_ref[:, C_WIDTH:2 * C_WIDTH]))
    u = _gelu(_dot(h, w_ref[:, 0:C_WIDTH]))
    qd_ref[...] = (_dot(h, w_ref[:, base:base + D_WIDTH]) * (HEAD_DIM ** -0.5 * LOG2E)).astype(BF16)
    kd_ref[...] = _dot(h, w_ref[:, base + D_WIDTH:base + 2 * D_WIDTH]).astype(BF16)
    vd_ref[...] = _dot(h, w_ref[:, base + 2 * D_WIDTH:base + 3 * D_WIDTH]).astype(BF16)
    mu = jnp.mean(v, -1, keepdims=True)
    vc = v - mu
    var = jnp.mean(vc * vc, -1, keepdims=True)
    vn = (vc * lax.rsqrt(var + EPS) * vg_ref[...]).astype(BF16)
    lane = lax.broadcasted_iota(jnp.int32, (C_CHUNK, LANES), 1)
    group0 = lane < C_GROUP_W
    for c in range(x_ref.shape[0] // C_CHUNK):
        rs = slice(C_CHUNK * c, C_CHUNK * (c + 1))
        for pair in range(C_GROUPS // 2):
            ls = slice(LANES * pair, LANES * (pair + 1))
            vp = vn[rs, ls]
            sv = jnp.where(group0, _dot(ws_ref[2 * pair], vp), _dot(ws_ref[2 * pair + 1], vp)) + bs_ref[:, ls]
            c_ref[rs, ls] = (u[rs, ls] * sv).astype(BF16)


def _odd_in(x2, g, w_in, vg, ws, bs_tile):
    t, d = x2.shape
    tm = IN_TILE
    row = lambda n: pl.BlockSpec((tm, n), lambda i: (i, 0))
    out_shapes = [jax.ShapeDtypeStruct((t, n), BF16) for n in (C_WIDTH, D_WIDTH, D_WIDTH, D_WIDTH)]
    return pl.pallas_call(
        _odd_in_kernel,
        grid=(t // tm,),
        in_specs=[row(d), _const_spec(g.shape), _const_spec(w_in.shape), _const_spec(vg.shape),
                  _const_spec(ws.shape), _const_spec(bs_tile.shape)],
        out_specs=[row(s.shape[1]) for s in out_shapes],
        out_shape=out_shapes,
        compiler_params=_params(("parallel",)),
        name="odd_in",
    )(x2, g, w_in, vg, ws, bs_tile)


def _na_geometry(rows):
    kr = min(NA_ROWS, rows)
    nunit = rows // NA_QROWS
    kstart = [int(np.clip(NA_QROWS * j - kr // 2, 0, rows - NA_KROWS)) for j in range(nunit)]
    keys = [(kstart[j] - NA_QROWS * j,
             tuple(int(np.clip(i - kr // 2, 0, rows - kr)) - i for i in range(NA_QROWS * j, NA_QROWS * (j + 1))))
            for j in range(nunit)]
    patterns = sorted(set(keys), key=keys.index)
    return kstart, [patterns.index(k) for k in keys], patterns


def _na_rows(rows):
    _, _, patterns = _na_geometry(rows)
    lo = min(p[0] for p in patterns) - (NA_QROWS - 1) + NA_ROWS - 1
    hi = max(p[0] for p in patterns) + NA_KROWS - 1 + NA_ROWS - 1
    return min(lo, 0), max(hi, 2 * NA_ROWS - 2)


def _na_bias(rpb, rows):
    lo, hi = _na_rows(rows)
    qc = np.arange(GRID_W)[:, None]
    kc = np.arange(GRID_W)[None, :]
    cstart = np.clip(qc - NA_COLS // 2, 0, GRID_W - NA_COLS)
    col_ok = (kc >= cstart) & (kc < cstart + NA_COLS)
    rpb = rpb.astype(F32) * LOG2E
    gap = jnp.zeros(rpb.shape[:-1] + (2 * GRID_W - (2 * NA_COLS - 1),), F32)
    by_col = _toeplitz(jnp.concatenate([rpb[..., NA_COLS - 1:], gap, rpb[..., :NA_COLS - 1]], -1), GRID_W, GRID_W)
    by_col = jnp.where(col_ok[None, None], by_col, NEG_INF)
    by_col = jnp.pad(by_col, ((0, 0), (-lo, hi - (2 * NA_ROWS - 2)), (0, 0), (0, 0)))
    return jnp.concatenate([by_col[:, :-1], by_col[:, 1:]], -1)


def _na_kernel(q_ref, k_ref, v_ref, b_ref, o_ref, qq, s0, s1, p0, p1, i0, i1, *, rows):
    s_buf, p_buf, inv_buf = (s0, s1), (p0, p1), (i0, i1)
    kstart, pattern, patterns = _na_geometry(rows)
    nq = NA_QROWS * GRID_W
    nk = NA_KROWS * GRID_W
    assert 2 * GRID_W == LANES and NA_KROWS % 2 == 0
    head0 = lax.broadcasted_iota(jnp.int32, (nq, LANES), 1) < HEAD_DIM
    for j in range(rows // NA_QROWS):
        t = q_ref[0, nq * j:nq * (j + 1), :].astype(F32)
        qq[2 * nq * j:2 * nq * j + nq, :] = jnp.where(head0, t, 0.0).astype(BF16)
        qq[2 * nq * j + nq:2 * nq * (j + 1), :] = jnp.where(head0, 0.0, t).astype(BF16)

    lo, _ = _na_rows(rows)
    kr = min(NA_ROWS, rows)
    low_half = lax.broadcasted_iota(jnp.int32, (GRID_W, LANES), 1) < GRID_W

    def bias_tile(h, j):
        kstart_rel, r0_rel = patterns[pattern[j]]
        tile_rows = []
        for a in range(NA_QROWS):
            blocks = []
            for ka in range(0, NA_KROWS, 2):
                krel = kstart_rel + ka - a
                ok = [r0_rel[a] <= krel + i < r0_rel[a] + kr for i in range(2)]
                if not any(ok):
                    blocks.append(jnp.full((GRID_W, LANES), NEG_INF, F32))
                    continue
                blk = b_ref[h, krel + NA_ROWS - 1 - lo]
                if not ok[0]:
                    blk = jnp.where(low_half, NEG_INF, blk)
                if not ok[1]:
                    blk = jnp.where(low_half, blk, NEG_INF)
                blocks.append(blk)
            tile_rows.append(jnp.concatenate(blocks, axis=1))
        return jnp.concatenate(tile_rows, axis=0)

    def window(j):
        return slice(nq * j, nq * (j + 1)), slice(GRID_W * kstart[j], GRID_W * kstart[j] + nk)

    def scores(j, slot):
        _, ks = window(j)
        bias = jnp.concatenate([bias_tile(0, j), bias_tile(1, j)], axis=0)
        s_buf[slot][...] = _dot_nt(qq[2 * nq * j:2 * nq * (j + 1), :], k_ref[0, ks, :]) + bias

    def softmax(j, slot):
        s = s_buf[slot][...]
        m = jnp.max(s, -1, keepdims=True)
        p = jnp.exp2(s - m)
        p_buf[slot][...] = p.astype(BF16)
        den = jnp.sum(p, -1, keepdims=True)
        inv_buf[slot][...] = 1.0 / jnp.where(head0, den[0:nq, :], den[nq:2 * nq, :])

    def values(j, slot):
        qs, ks = window(j)
        o2 = _dot(p_buf[slot][...], v_ref[0, ks, :])
        o_ref[0, qs, :] = (jnp.where(head0, o2[0:nq, :], o2[nq:2 * nq, :]) * inv_buf[slot][...]).astype(BF16)

    _pipeline3(rows // NA_QROWS, scores, softmax, values)


def _neighbourhood(qd, kd, vd, bias):
    b, s, _ = qd.shape
    npair = D_HEADS // 2
    tok = pl.BlockSpec((1, s, LANES), lambda hp, bi: (bi, 0, hp))
    return pl.pallas_call(
        functools.partial(_na_kernel, rows=s // GRID_W),
        grid=(npair, b),
        in_specs=[tok, tok, tok, pl.BlockSpec((2,) + bias.shape[1:], lambda hp, bi: (hp, 0, 0, 0))],
        out_specs=tok,
        out_shape=jax.ShapeDtypeStruct((b, s, D_WIDTH), BF16),
        scratch_shapes=([pltpu.VMEM((2 * s, LANES), BF16)]
                        + [pltpu.VMEM((2 * NA_QROWS * GRID_W, NA_KROWS * GRID_W), F32)] * 2
                        + [pltpu.VMEM((2 * NA_QROWS * GRID_W, NA_KROWS * GRID_W), BF16)] * 2
                        + [pltpu.VMEM((NA_QROWS * GRID_W, LANES), F32)] * 2),
        compiler_params=_params(("parallel", "arbitrary")),
        name="neighbourhood_attn",
    )(qd, kd, vd, bias)


def _mix_ffn_kernel(x_ref, a_ref, b_ref, wo_ref, g_ref, wgu_ref, wd_ref, fg_ref, o_ref, *, final, nchunk):
    half = a_ref.shape[1]
    x = x_ref[...] + _dot(a_ref[...], wo_ref[0:half, :]) + _dot(b_ref[...], wo_ref[half:2 * half, :])
    h = _rms(x, g_ref[...]).astype(BF16)
    hidden = wd_ref.shape[0]
    hc = hidden // nchunk
    acc = x
    for c in range(nchunk):
        gate = _dot(h, wgu_ref[:, hc * c:hc * (c + 1)])
        up = _dot(h, wgu_ref[:, hidden + hc * c:hidden + hc * (c + 1)])
        act = (gate * (1.0 / (1.0 + jnp.exp(-gate))) * up).astype(BF16)
        acc = acc + _dot(act, wd_ref[hc * c:hc * (c + 1), :])
    o_ref[...] = _rms(acc, fg_ref[...]) if final else acc


def _mix_ffn(x2, a, b, wo, g, wgu, wd, fg, final):
    t, d = x2.shape
    tm = ROW_TILE
    row = lambda n: pl.BlockSpec((tm, n), lambda i: (i, 0))
    return pl.pallas_call(
        functools.partial(_mix_ffn_kernel, final=final, nchunk=1),
        grid=(t // tm,),
        in_specs=[row(d), row(a.shape[1]), row(b.shape[1]), _const_spec(wo.shape), _const_spec(g.shape),
                  _const_spec(wgu.shape), _const_spec(wd.shape), _const_spec(fg.shape)],
        out_specs=row(d),
        out_shape=jax.ShapeDtypeStruct((t, d), F32),
        compiler_params=_params(("parallel",)),
        name="mix_ffn",
    )(x2, a, b, wo, g, wgu, wd, fg)


def _rope_tables(seq):
    pos = jnp.arange(seq, dtype=F32)
    inv = 1.0 / (ROPE_BASE ** (jnp.arange(0, B_ROPE, 2, dtype=F32) / B_ROPE))
    ang = pos[:, None] * inv[None, :]
    cos, sin = jnp.cos(ang), jnp.sin(ang)
    hf = B_ROPE // 2
    ones = jnp.ones((seq, B_NOPE), F32)
    z = lambda n: jnp.zeros((seq, n), F32)
    tail = LANES - B_NOPE - B_ROPE
    c = jnp.concatenate([ones, cos, cos, z(tail)], 1)
    s1 = jnp.concatenate([z(B_NOPE), -sin, z(hf), z(tail)], 1)
    s2 = jnp.concatenate([z(B_NOPE), z(hf), sin, z(tail)], 1)
    return c, s1, s2


def _pad_heads(w, heads, width):
    k = w.shape[0]
    return jnp.pad(w.reshape(k, heads, width), ((0, 0), (0, 0), (0, LANES - width))).reshape(k, heads * LANES)


def kernel(x, t5_bias, norm_mix, norm_ffn, ev_w_in, ev_q_gain, ev_kv_gain, ev_w_uq, ev_w_ukv, ev_w_out,
           od_w_in, od_v_gain, od_w_s, od_b_s, od_rpb, od_w_out, ffn_w_gu, ffn_w_down, final_gain):
    bsz, seq, d = x.shape
    depth = norm_mix.shape[0]
    t = bsz * seq
    assert seq % ROW_TILE == 0 and seq % IN_TILE == 0 and seq % GRID_W == 0 and IN_TILE % C_CHUNK == 0
    x2 = x.reshape(t, d)
    rope_c, rope_s1, rope_s2 = _rope_tables(seq)
    dil_bias = [_dil_bias(t5_bias, seq, dil) for _, dil in A_BRANCHES]
    row = lambda v: v.reshape(1, -1).astype(F32)
    tok3 = lambda a: a.reshape(bsz, seq, a.shape[-1])
    tok2 = lambda a: a.reshape(t, a.shape[-1])

    for layer in range(depth):
        j = layer // 2
        if layer % 2 == 0:
            w_in = ev_w_in[j]
            o2 = 3 * A_WIDTH + B_Q_RANK + B_KV_RANK
            w_in = jnp.concatenate([w_in[:, :o2], jnp.zeros((d, B_NOPE), F32), w_in[:, o2:],
                                    jnp.zeros((d, LANES - B_NOPE - B_ROPE), F32)], 1).astype(BF16)
            wuq = _pad_heads(ev_w_uq[j], B_HEADS, B_NOPE + B_ROPE).astype(BF16)
            wukv = ev_w_ukv[j].reshape(B_KV_RANK, B_HEADS, B_NOPE + B_V)
            wk = _pad_heads(wukv[:, :, :B_NOPE].reshape(B_KV_RANK, -1), B_HEADS, B_NOPE).astype(BF16)
            wv = wukv[:, :, B_NOPE:].reshape(B_KV_RANK, -1).astype(BF16)
            qa, ka, va, q, k, v = _even_in(x2, row(norm_mix[layer]), w_in, row(ev_q_gain[j]), row(ev_kv_gain[j]),
                                           wuq, wk, wv, rope_c, rope_s1, rope_s2, seq)
            m0 = tok2(_dilated(tok3(qa), tok3(ka), tok3(va), dil_bias))
            m1 = tok2(_mla(tok3(q), tok3(k), tok3(v)))
            wo = ev_w_out[j]
        else:
            bs_tile = jnp.repeat(jnp.transpose(od_b_s[j]), C_GROUP_W, axis=1).astype(F32)
            m0, qd, kd, vd = _odd_in(x2, row(norm_mix[layer]), od_w_in[j].astype(BF16), row(od_v_gain[j]),
                                     od_w_s[j].astype(BF16), bs_tile)
            m1 = tok2(_neighbourhood(tok3(qd), tok3(kd), tok3(vd), _na_bias(od_rpb[j], seq // GRID_W)))
            wo = od_w_out[j]
        x2 = _mix_ffn(x2, m0, m1, wo.astype(BF16), row(norm_ffn[layer]), ffn_w_gu[layer].astype(BF16),
                      ffn_w_down[layer].astype(BF16), row(final_gain), final=layer == depth - 1)
    return x2.reshape(bsz, seq, d)
```

```python
import functools
import math

import jax
import jax.numpy as jnp
import numpy as np
from jax import lax
from jax.experimental import pallas as pl
from jax.experimental.pallas import tpu as pltpu

HEAD_DIM = 64
EPS = 1e-6
NEG_INF = -1e30
LOG2E = math.log2(math.e)
A_HEADS = 8
A_BRANCHES = ((128, 1), (512, 4), (2048, 16))
T5_BUCKETS = 32
T5_MAX_DIST = 1024
B_HEADS = 8
B_Q_RANK = 768
B_KV_RANK = 256
B_NOPE = 64
B_ROPE = 32
B_V = 64
ROPE_BASE = 10000.0
C_GROUPS = 8
C_GROUP_W = 64
C_CHUNK = 128
C_WIDTH = C_GROUPS * C_GROUP_W
D_HEADS = 8
GRID_W = 64
NA_ROWS = 8
NA_COLS = 16
A_WIDTH = A_HEADS * HEAD_DIM
D_WIDTH = D_HEADS * HEAD_DIM

LANES = 128
VMEM_LIMIT = 56 * 1024 * 1024
ROW_TILE = 512
IN_TILE = 1024
MLA_Q_TILE = 256
DIL_Q = 128
DIL_HALF = 64
DIL_GROUP = 1
NA_QROWS = 1
NA_KROWS = 8

BF16 = jnp.bfloat16
F32 = jnp.float32


def _dot(a, b):
    return jnp.dot(a, b, preferred_element_type=F32)


def _dot_nt(a, b):
    return lax.dot_general(a, b, (((1,), (1,)), ((), ())), preferred_element_type=F32)


def _rms(x, g):
    return x * lax.rsqrt(jnp.mean(x * x, -1, keepdims=True) + EPS) * g


def _const_spec(shape):
    nd = len(shape)
    return pl.BlockSpec(shape, lambda *_: (0,) * nd, pipeline_mode=pl.Buffered(1))


def _params(sem):
    return pltpu.CompilerParams(dimension_semantics=sem, vmem_limit_bytes=VMEM_LIMIT)


def _pipeline3(njobs, stage_a, stage_b, stage_c):
    for step in range(njobs + 2):
        for lag, stage in enumerate((stage_a, stage_b, stage_c)):
            j = step - lag
            if 0 <= j < njobs:
                stage(j, j % 2)


def _rope(t, c, s1, s2):
    return t * c + pltpu.roll(t, LANES - B_ROPE // 2, 1) * s1 + pltpu.roll(t, B_ROPE // 2, 1) * s2


def _even_in_kernel(x_ref, g_ref, w_ref, qg_ref, kvg_ref, wuq_ref, wk_ref, wv_ref, c_ref, s1_ref, s2_ref,
                    qa_ref, ka_ref, va_ref, q_ref, k_ref, v_ref):
    h = _rms(x_ref[...], g_ref[...]).astype(BF16)
    aw = A_WIDTH
    o0 = 3 * aw
    o1 = o0 + B_Q_RANK
    o2 = o1 + B_KV_RANK
    cq = _rms(_dot(h, w_ref[:, o0:o1]), qg_ref[...]).astype(BF16)
    ckv = _rms(_dot(h, w_ref[:, o1:o2]), kvg_ref[...]).astype(BF16)
    c, s1, s2 = c_ref[...], s1_ref[...], s2_ref[...]
    kpe = _rope(_dot(h, w_ref[:, o2:o2 + LANES]), c, s1, s2)
    qa_ref[...] = _dot(h, w_ref[:, 0:aw]).astype(BF16)
    q = _dot(cq, wuq_ref[...])
    kn = _dot(ckv, wk_ref[...])
    v_ref[...] = _dot(ckv, wv_ref[...]).astype(BF16)
    ka_ref[...] = _dot(h, w_ref[:, aw:2 * aw]).astype(BF16)
    qscale = (B_NOPE + B_ROPE) ** -0.5 * LOG2E
    for hh in range(B_HEADS):
        sl = slice(LANES * hh, LANES * (hh + 1))
        q_ref[:, sl] = (_rope(q[:, sl], c, s1, s2) * qscale).astype(BF16)
        k_ref[:, sl] = (kn[:, sl] + kpe).astype(BF16)
    va_ref[...] = _dot(h, w_ref[:, 2 * aw:3 * aw]).astype(BF16)


def _even_in(x2, g, w_in, qg, kvg, wuq, wk, wv, rope_c, rope_s1, rope_s2, seq):
    t, d = x2.shape
    tm = IN_TILE
    nseq = seq // tm
    row = lambda n: pl.BlockSpec((tm, n), lambda i: (i, 0))
    pos = pl.BlockSpec((tm, LANES), lambda i: (i % nseq, 0))
    qk_w = B_HEADS * LANES
    out_shapes = [jax.ShapeDtypeStruct((t, n), BF16) for n in (A_WIDTH, A_WIDTH, A_WIDTH, qk_w, qk_w, B_HEADS * B_V)]
    return pl.pallas_call(
        _even_in_kernel,
        grid=(t // tm,),
        in_specs=[row(d), _const_spec(g.shape), _const_spec(w_in.shape), _const_spec(qg.shape),
                  _const_spec(kvg.shape), _const_spec(wuq.shape), _const_spec(wk.shape), _const_spec(wv.shape),
                  pos, pos, pos],
        out_specs=[row(s.shape[1]) for s in out_shapes],
        out_shape=out_shapes,
        compiler_params=_params(("parallel",)),
        name="even_in",
    )(x2, g, w_in, qg, kvg, wuq, wk, wv, rope_c, rope_s1, rope_s2)


def _mla_kernel(q_ref, k_ref, v_ref, o_ref, s_a, s_b, p_a, p_b, inv_a, inv_b):
    seq = q_ref.shape[1]
    sub = MLA_Q_TILE
    head0 = lax.broadcasted_iota(jnp.int32, (sub, LANES), 1) < B_V
    s_buf, p_buf, inv_buf = (s_a, s_b), (p_a, p_b), (inv_a, inv_b)

    def scores(t, slot):
        for h in range(2):
            sl = slice(LANES * h, LANES * (h + 1))
            s_buf[slot][sub * h:sub * (h + 1), :] = _dot_nt(q_ref[0, sub * t:sub * (t + 1), sl], k_ref[0, :, sl])

    def softmax(t, slot):
        s = s_buf[slot][...]
        m = jnp.max(s, -1, keepdims=True)
        p = jnp.exp2(s - m)
        p_buf[slot][...] = p.astype(BF16)
        den = jnp.sum(p, -1, keepdims=True)
        inv_buf[slot][...] = 1.0 / jnp.where(head0, den[0:sub, :], den[sub:2 * sub, :])

    def values(t, slot):
        o2 = _dot(p_buf[slot][...], v_ref[0])
        o = jnp.where(head0, o2[0:sub, :], o2[sub:2 * sub, :]) * inv_buf[slot][...]
        o_ref[0, sub * t:sub * (t + 1), :] = o.astype(BF16)

    _pipeline3(seq // sub, scores, softmax, values)


def _mla(q, k, v):
    b, s, _ = q.shape
    sub = MLA_Q_TILE
    return pl.pallas_call(
        _mla_kernel,
        grid=(b, B_HEADS // 2),
        in_specs=[pl.BlockSpec((1, s, 2 * LANES), lambda bi, hp: (bi, 0, hp)),
                  pl.BlockSpec((1, s, 2 * LANES), lambda bi, hp: (bi, 0, hp)),
                  pl.BlockSpec((1, s, LANES), lambda bi, hp: (bi, 0, hp))],
        out_specs=pl.BlockSpec((1, s, LANES), lambda bi, hp: (bi, 0, hp)),
        out_shape=jax.ShapeDtypeStruct((b, s, B_HEADS * B_V), BF16),
        scratch_shapes=([pltpu.VMEM((2 * sub, s), F32)] * 2 + [pltpu.VMEM((2 * sub, s), BF16)] * 2
                        + [pltpu.VMEM((sub, LANES), F32)] * 2),
        compiler_params=_params(("parallel", "parallel")),
        name="mla_attn",
    )(q, k, v)


def _t5_bucket(rel):
    nb = T5_BUCKETS // 2
    max_exact = nb // 2
    n = np.abs(rel)
    large = max_exact + (np.log(np.maximum(n, 1) / max_exact) / math.log(T5_MAX_DIST / max_exact)
                         * (nb - max_exact)).astype(np.int64)
    large = np.minimum(large, nb - 1)
    return ((rel > 0) * nb + np.where(n < max_exact, n, large)).astype(np.int32)


def _dil_geometry(seq, dil):
    length = seq // dil
    assert length % DIL_Q == 0
    if length == DIL_Q:
        return length, DIL_Q, (0,)
    return length, 2 * DIL_Q, (0, DIL_HALF, DIL_Q)


def _toeplitz(vec, nrow, ncol):
    p = vec.shape[-1]
    flat = jnp.tile(vec, (1,) * (vec.ndim - 1) + (nrow,))[..., :nrow * (p - 1)]
    return flat.reshape(vec.shape[:-1] + (nrow, p - 1))[..., :ncol]


def _dil_bias(t5_bias, seq, dil):
    _, kwid, shifts = _dil_geometry(seq, dil)
    period = DIL_Q + kwid
    k = np.arange(period)
    k = np.where(k < kwid, k, k - period)
    off = k[None, :] - np.asarray(shifts)[:, None]
    valid = np.abs(off) <= DIL_HALF
    vec = jnp.transpose(t5_bias[_t5_bucket(off * dil)], (2, 0, 1)).astype(F32)
    vec = jnp.where(valid[None], vec * LOG2E, NEG_INF)
    bias = _toeplitz(vec, DIL_Q, kwid).reshape(A_HEADS // 2, 2, len(shifts), DIL_Q, kwid)
    return jnp.transpose(bias, (0, 2, 1, 3, 4)).reshape(A_HEADS // 2 * len(shifts), 2 * DIL_Q, kwid)


def _dil_kernel(q_ref, k_ref, v_ref, b1_ref, b4_ref, b16_ref, o_ref,
                f0, f1, f2, f3, f4, f5, q1, q4, k4, v4, q16, k16, v16,
                ob1, ob4, ob16, lb1, lb4, lb16, s0, s1, p0, p1, i0, i1, *, seq):
    s_buf, p_buf, inv_buf = (s0, s1), (p0, p1), (i0, i1)
    assert tuple(d for _, d in A_BRANCHES) == (1, 4, 16)
    qscale = HEAD_DIM ** -0.5 * LOG2E
    len4, len16 = seq // 4, seq // 16

    def put_q(dst, row0, t):
        head0 = lax.broadcasted_iota(jnp.int32, t.shape, 1) < HEAD_DIM
        t = t * qscale
        a, b = jnp.where(head0, t, 0.0).astype(BF16), jnp.where(head0, 0.0, t).astype(BF16)
        for i in range(t.shape[0] // DIL_Q):
            us = slice(DIL_Q * i, DIL_Q * (i + 1))
            dst[2 * row0 + 2 * DIL_Q * i:2 * row0 + 2 * DIL_Q * i + DIL_Q, :] = a[us, :]
            dst[2 * row0 + 2 * DIL_Q * i + DIL_Q:2 * row0 + 2 * DIL_Q * (i + 1), :] = b[us, :]

    for c in range(4):
        rs = slice(len4 * c, len4 * (c + 1))
        t = q_ref[0, rs, :].astype(F32)
        f0[rs, :] = t
        put_q(q1, len4 * c, t)
        f1[rs, :] = k_ref[0, rs, :].astype(F32)
        f2[rs, :] = v_ref[0, rs, :].astype(F32)
    for r in range(4):
        rs = slice(len4 * r, len4 * (r + 1))
        t = f0[pl.ds(r, len4, stride=4), :]
        f3[rs, :] = t
        put_q(q4, len4 * r, t)
        t = f1[pl.ds(r, len4, stride=4), :]
        f4[rs, :] = t
        k4[rs, :] = t.astype(BF16)
        t = f2[pl.ds(r, len4, stride=4), :]
        f5[rs, :] = t
        v4[rs, :] = t.astype(BF16)
    for r4 in range(4):
        for rp in range(4):
            rs = slice(len16 * (r4 + 4 * rp), len16 * (r4 + 4 * rp + 1))
            src = pl.ds(len4 * r4 + rp, len16, stride=4)
            put_q(q16, len16 * (r4 + 4 * rp), f3[src, :])
            k16[rs, :] = f4[src, :].astype(BF16)
            v16[rs, :] = f5[src, :].astype(BF16)

    branches = ((1, q1, lambda ks: k_ref[0, ks, :], lambda ks: v_ref[0, ks, :], b1_ref, ob1, lb1),
                (4, q4, lambda ks: k4[ks, :], lambda ks: v4[ks, :], b4_ref, ob4, lb4),
                (16, q16, lambda ks: k16[ks, :], lambda ks: v16[ks, :], b16_ref, ob16, lb16))
    ngroup = seq // (DIL_Q * DIL_GROUP)
    head0 = lax.broadcasted_iota(jnp.int32, (DIL_Q, LANES), 1) < HEAD_DIM

    def geom(job):
        dil = branches[job // ngroup][0]
        length, kwid, shifts = _dil_geometry(seq, dil)
        nqb = length // DIL_Q
        units = []
        for i in range(DIL_GROUP):
            u = (job % ngroup) * DIL_GROUP + i
            seg, qb = divmod(u, nqb)
            kstart = seg * length + min(max(qb * DIL_Q - DIL_HALF, 0), length - kwid)
            case = 0 if len(shifts) == 1 else (0 if qb == 0 else 2 if qb == nqb - 1 else 1)
            if dil == 1:
                out = slice(DIL_Q * u, DIL_Q * (u + 1))
            elif dil == 4:
                out = pl.ds(seg + dil * DIL_Q * qb, DIL_Q, stride=4)
            else:
                out = pl.ds(len4 * (u % 4) + u // 4, DIL_Q, stride=4)
            units.append((u, slice(kstart, kstart + kwid), case, out))
        return kwid, units

    def pair_rows(i):
        return slice(2 * DIL_Q * i, 2 * DIL_Q * (i + 1))

    def scores(job, slot):
        _, q_all, kload, _, b_ref, _, _ = branches[job // ngroup]
        kwid, units = geom(job)
        for i, (u, ks, case, _) in enumerate(units):
            s_buf[slot][pair_rows(i), 0:kwid] = _dot_nt(q_all[pair_rows(u), :], kload(ks)) + b_ref[case]

    def softmax(job, slot):
        lb = branches[job // ngroup][6]
        kwid, units = geom(job)
        s = s_buf[slot][:, 0:kwid]
        m = jnp.max(s, -1, keepdims=True)
        p = jnp.exp2(s - m)
        p_buf[slot][:, 0:kwid] = p.astype(BF16)
        den = jnp.sum(p, -1, keepdims=True)
        for i, (_, _, _, out) in enumerate(units):
            top, bot = slice(2 * DIL_Q * i, 2 * DIL_Q * i + DIL_Q), slice(2 * DIL_Q * i + DIL_Q, 2 * DIL_Q * (i + 1))
            den_c = jnp.where(head0, den[top, :], den[bot, :])
            inv_buf[slot][DIL_Q * i:DIL_Q * (i + 1), :] = 1.0 / den_c
            lb[out, :] = jnp.where(head0, m[top, :], m[bot, :]) + jnp.log2(den_c)

    def values(job, slot):
        _, _, _, vload, _, ob, _ = branches[job // ngroup]
        kwid, units = geom(job)
        for i, (_, ks, _, out) in enumerate(units):
            o2 = _dot(p_buf[slot][pair_rows(i), 0:kwid], vload(ks))
            o = jnp.where(head0, o2[0:DIL_Q, :], o2[DIL_Q:2 * DIL_Q, :])
            ob[out, :] = o * inv_buf[slot][DIL_Q * i:DIL_Q * (i + 1), :]

    _pipeline3(len(branches) * ngroup, scores, softmax, values)

    for src16, dst in ((ob16, f0), (lb16, f1)):
        for r in range(4):
            dst[pl.ds(r, len4, stride=4), :] = src16[len4 * r:len4 * (r + 1), :]

    for c in range(seq // 256):
        sl = slice(256 * c, 256 * (c + 1))
        l0, l1, l2 = lb1[sl, :], lb4[sl, :], f1[sl, :]
        m = jnp.maximum(jnp.maximum(l0, l1), l2)
        w0, w1, w2 = jnp.exp2(l0 - m), jnp.exp2(l1 - m), jnp.exp2(l2 - m)
        mix = (w0 * ob1[sl, :] + w1 * ob4[sl, :] + w2 * f0[sl, :]) / (w0 + w1 + w2)
        o_ref[0, sl, :] = mix.astype(BF16)


def _dilated(qa, ka, va, biases):
    b, s, _ = qa.shape
    npair = A_HEADS // 2
    tok = pl.BlockSpec((1, s, LANES), lambda hp, bi: (bi, 0, hp))

    def bias_spec(arr):
        n = arr.shape[0] // npair
        return pl.BlockSpec((n,) + arr.shape[1:], lambda hp, bi: (hp, 0, 0))

    return pl.pallas_call(
        functools.partial(_dil_kernel, seq=s),
        grid=(npair, b),
        in_specs=[tok, tok, tok] + [bias_spec(a) for a in biases],
        out_specs=tok,
        out_shape=jax.ShapeDtypeStruct((b, s, A_WIDTH), BF16),
        scratch_shapes=([pltpu.VMEM((s, LANES), F32)] * 6
                        + [pltpu.VMEM((n * s, LANES), BF16) for n in (2, 2, 1, 1, 2, 1, 1)]
                        + [pltpu.VMEM((s, LANES), F32)] * 6
                        + [pltpu.VMEM((2 * DIL_Q * DIL_GROUP, 2 * DIL_Q), F32)] * 2
                        + [pltpu.VMEM((2 * DIL_Q * DIL_GROUP, 2 * DIL_Q), BF16)] * 2
                        + [pltpu.VMEM((DIL_Q * DIL_GROUP, LANES), F32)] * 2),
        compiler_params=_params(("parallel", "arbitrary")),
        name="dilated_attn",
    )(qa, ka, va, *biases)


def _gelu(x):
    c = math.sqrt(2.0 / math.pi)
    return x * (0.5 * (1.0 + jnp.tanh(c * (x + 0.044715 * (x * x * x)))))


def _odd_in_kernel(x_ref, g_ref, w_ref, vg_ref, ws_ref, bs_ref, c_ref, qd_ref, kd_ref, vd_ref):
    h = _rms(x_ref[...], g_ref[...]).astype(BF16)
    base = 2 * C_WIDTH
    v = _gelu(_dot(h, w_ref[:, C_WIDTH:2 * C_WIDTH]))
    u = _gelu(_dot(h, w_ref[:, 0:C_WIDTH]))
    qd_ref[...] = (_dot(h, w_ref[:, base:base + D_WIDTH]) * (HEAD_DIM ** -0.5 * LOG2E)).astype(BF16)
    kd_ref[...] = _dot(h, w_ref[:, base + D_WIDTH:base + 2 * D_WIDTH]).astype(BF16)
    vd_ref[...] = _dot(h, w_ref[:, base + 2 * D_WIDTH:base + 3 * D_WIDTH]).astype(BF16)
    mu = jnp.mean(v, -1, keepdims=True)
    vc = v - mu
    var = jnp.mean(vc * vc, -1, keepdims=True)
    vn = (vc * lax.rsqrt(var + EPS) * vg_ref[...]).astype(BF16)
    lane = lax.broadcasted_iota(jnp.int32, (C_CHUNK, LANES), 1)
    group0 = lane < C_GROUP_W
    for c in range(x_ref.shape[0] // C_CHUNK):
        rs = slice(C_CHUNK * c, C_CHUNK * (c + 1))
        for pair in range(C_GROUPS // 2):
            ls = slice(LANES * pair, LANES * (pair + 1))
            vp = vn[rs, ls]
            sv = jnp.where(group0, _dot(ws_ref[2 * pair], vp), _dot(ws_ref[2 * pair + 1], vp)) + bs_ref[:, ls]
            c_ref[rs, ls] = (u[rs, ls] * sv).astype(BF16)


def _odd_in(x2, g, w_in, vg, ws, bs_tile):
    t, d = x2.shape
    tm = IN_TILE
    row = lambda n: pl.BlockSpec((tm, n), lambda i: (i, 0))
    out_shapes = [jax.ShapeDtypeStruct((t, n), BF16) for n in (C_WIDTH, D_WIDTH, D_WIDTH, D_WIDTH)]
    return pl.pallas_call(
        _odd_in_kernel,
        grid=(t // tm,),
        in_specs=[row(d), _const_spec(g.shape), _const_spec(w_in.shape), _const_spec(vg.shape),
                  _const_spec(ws.shape), _const_spec(bs_tile.shape)],
        out_specs=[row(s.shape[1]) for s in out_shapes],
        out_shape=out_shapes,
        compiler_params=_params(("parallel",)),
        name="odd_in",
    )(x2, g, w_in, vg, ws, bs_tile)


def _na_geometry(rows):
    kr = min(NA_ROWS, rows)
    nunit = rows // NA_QROWS
    kstart = [int(np.clip(NA_QROWS * j - kr // 2, 0, rows - NA_KROWS)) for j in range(nunit)]
    keys = [(kstart[j] - NA_QROWS * j,
             tuple(int(np.clip(i - kr // 2, 0, rows - kr)) - i for i in range(NA_QROWS * j, NA_QROWS * (j + 1))))
            for j in range(nunit)]
    patterns = sorted(set(keys), key=keys.index)
    return kstart, [patterns.index(k) for k in keys], patterns


def _na_rows(rows):
    _, _, patterns = _na_geometry(rows)
    lo = min(p[0] for p in patterns) - (NA_QROWS - 1) + NA_ROWS - 1
    hi = max(p[0] for p in patterns) + NA_KROWS - 1 + NA_ROWS - 1
    return min(lo, 0), max(hi, 2 * NA_ROWS - 2)


def _na_bias(rpb, rows):
    lo, hi = _na_rows(rows)
    qc = np.arange(GRID_W)[:, None]
    kc = np.arange(GRID_W)[None, :]
    cstart = np.clip(qc - NA_COLS // 2, 0, GRID_W - NA_COLS)
    col_ok = (kc >= cstart) & (kc < cstart + NA_COLS)
    rpb = rpb.astype(F32) * LOG2E
    gap = jnp.zeros(rpb.shape[:-1] + (2 * GRID_W - (2 * NA_COLS - 1),), F32)
    by_col = _toeplitz(jnp.concatenate([rpb[..., NA_COLS - 1:], gap, rpb[..., :NA_COLS - 1]], -1), GRID_W, GRID_W)
    by_col = jnp.where(col_ok[None, None], by_col, NEG_INF)
    by_col = jnp.pad(by_col, ((0, 0), (-lo, hi - (2 * NA_ROWS - 2)), (0, 0), (0, 0)))
    return jnp.concatenate([by_col[:, :-1], by_col[:, 1:]], -1)


def _na_kernel(q_ref, k_ref, v_ref, b_ref, o_ref, qq, s0, s1, p0, p1, i0, i1, *, rows):
    s_buf, p_buf, inv_buf = (s0, s1), (p0, p1), (i0, i1)
    kstart, pattern, patterns = _na_geometry(rows)
    nq = NA_QROWS * GRID_W
    nk = NA_KROWS * GRID_W
    assert 2 * GRID_W == LANES and NA_KROWS % 2 == 0
    head0 = lax.broadcasted_iota(jnp.int32, (nq, LANES), 1) < HEAD_DIM
    for j in range(rows // NA_QROWS):
        t = q_ref[0, nq * j:nq * (j + 1), :].astype(F32)
        qq[2 * nq * j:2 * nq * j + nq, :] = jnp.where(head0, t, 0.0).astype(BF16)
        qq[2 * nq * j + nq:2 * nq * (j + 1), :] = jnp.where(head0, 0.0, t).astype(BF16)

    lo, _ = _na_rows(rows)
    kr = min(NA_ROWS, rows)
    low_half = lax.broadcasted_iota(jnp.int32, (GRID_W, LANES), 1) < GRID_W

    def bias_tile(h, j):
        kstart_rel, r0_rel = patterns[pattern[j]]
        tile_rows = []
        for a in range(NA_QROWS):
            blocks = []
            for ka in range(0, NA_KROWS, 2):
                krel = kstart_rel + ka - a
                ok = [r0_rel[a] <= krel + i < r0_rel[a] + kr for i in range(2)]
                if not any(ok):
                    blocks.append(jnp.full((GRID_W, LANES), NEG_INF, F32))
                    continue
                blk = b_ref[h, krel + NA_ROWS - 1 - lo]
                if not ok[0]:
                    blk = jnp.where(low_half, NEG_INF, blk)
                if not ok[1]:
                    blk = jnp.where(low_half, blk, NEG_INF)
                blocks.append(blk)
            tile_rows.append(jnp.concatenate(blocks, axis=1))
        return jnp.concatenate(tile_rows, axis=0)

    def window(j):
        return slice(nq * j, nq * (j + 1)), slice(GRID_W * kstart[j], GRID_W * kstart[j] + nk)

    def scores(j, slot):
        _, ks = window(j)
        bias = jnp.concatenate([bias_tile(0, j), bias_tile(1, j)], axis=0)
        s_buf[slot][...] = _dot_nt(qq[2 * nq * j:2 * nq * (j + 1), :], k_ref[0, ks, :]) + bias

    def softmax(j, slot):
        s = s_buf[slot][...]
        m = jnp.max(s, -1, keepdims=True)
        p = jnp.exp2(s - m)
        p_buf[slot][...] = p.astype(BF16)
        den = jnp.sum(p, -1, keepdims=True)
        inv_buf[slot][...] = 1.0 / jnp.where(head0, den[0:nq, :], den[nq:2 * nq, :])

    def values(j, slot):
        qs, ks = window(j)
        o2 = _dot(p_buf[slot][...], v_ref[0, ks, :])
        o_ref[0, qs, :] = (jnp.where(head0, o2[0:nq, :], o2[nq:2 * nq, :]) * inv_buf[slot][...]).astype(BF16)

    _pipeline3(rows // NA_QROWS, scores, softmax, values)


def _neighbourhood(qd, kd, vd, bias):
    b, s, _ = qd.shape
    npair = D_HEADS // 2
    tok = pl.BlockSpec((1, s, LANES), lambda hp, bi: (bi, 0, hp))
    return pl.pallas_call(
        functools.partial(_na_kernel, rows=s // GRID_W),
        grid=(npair, b),
        in_specs=[tok, tok, tok, pl.BlockSpec((2,) + bias.shape[1:], lambda hp, bi: (hp, 0, 0, 0))],
        out_specs=tok,
        out_shape=jax.ShapeDtypeStruct((b, s, D_WIDTH), BF16),
        scratch_shapes=([pltpu.VMEM((2 * s, LANES), BF16)]
                        + [pltpu.VMEM((2 * NA_QROWS * GRID_W, NA_KROWS * GRID_W), F32)] * 2
                        + [pltpu.VMEM((2 * NA_QROWS * GRID_W, NA_KROWS * GRID_W), BF16)] * 2
                        + [pltpu.VMEM((NA_QROWS * GRID_W, LANES), F32)] * 2),
        compiler_params=_params(("parallel", "arbitrary")),
        name="neighbourhood_attn",
    )(qd, kd, vd, bias)


def _mix_ffn_kernel(x_ref, a_ref, b_ref, wo_ref, g_ref, wgu_ref, wd_ref, fg_ref, o_ref, *, final, nchunk):
    half = a_ref.shape[1]
    x = x_ref[...] + _dot(a_ref[...], wo_ref[0:half, :]) + _dot(b_ref[...], wo_ref[half:2 * half, :])
    h = _rms(x, g_ref[...]).astype(BF16)
    hidden = wd_ref.shape[0]
    hc = hidden // nchunk
    acc = x
    for c in range(nchunk):
        gate = _dot(h, wgu_ref[:, hc * c:hc * (c + 1)])
        up = _dot(h, wgu_ref[:, hidden + hc * c:hidden + hc * (c + 1)])
        act = (gate * (1.0 / (1.0 + jnp.exp(-gate))) * up).astype(BF16)
        acc = acc + _dot(act, wd_ref[hc * c:hc * (c + 1), :])
    o_ref[...] = _rms(acc, fg_ref[...]) if final else acc


def _mix_ffn(x2, a, b, wo, g, wgu, wd, fg, final):
    t, d = x2.shape
    tm = ROW_TILE
    row = lambda n: pl.BlockSpec((tm, n), lambda i: (i, 0))
    return pl.pallas_call(
        functools.partial(_mix_ffn_kernel, final=final, nchunk=1),
        grid=(t // tm,),
        in_specs=[row(d), row(a.shape[1]), row(b.shape[1]), _const_spec(wo.shape), _const_spec(g.shape),
                  _const_spec(wgu.shape), _const_spec(wd.shape), _const_spec(fg.shape)],
        out_specs=row(d),
        out_shape=jax.ShapeDtypeStruct((t, d), F32),
        compiler_params=_params(("parallel",)),
        name="mix_ffn",
    )(x2, a, b, wo, g, wgu, wd, fg)


def _rope_tables(seq):
    pos = jnp.arange(seq, dtype=F32)
    inv = 1.0 / (ROPE_BASE ** (jnp.arange(0, B_ROPE, 2, dtype=F32) / B_ROPE))
    ang = pos[:, None] * inv[None, :]
    cos, sin = jnp.cos(ang), jnp.sin(ang)
    hf = B_ROPE // 2
    ones = jnp.ones((seq, B_NOPE), F32)
    z = lambda n: jnp.zeros((seq, n), F32)
    tail = LANES - B_NOPE - B_ROPE
    c = jnp.concatenate([ones, cos, cos, z(tail)], 1)
    s1 = jnp.concatenate([z(B_NOPE), -sin, z(hf), z(tail)], 1)
    s2 = jnp.concatenate([z(B_NOPE), z(hf), sin, z(tail)], 1)
    return c, s1, s2


def _pad_heads(w, heads, width):
    k = w.shape[0]
    return jnp.pad(w.reshape(k, heads, width), ((0, 0), (0, 0), (0, LANES - width))).reshape(k, heads * LANES)


def kernel(x, t5_bias, norm_mix, norm_ffn, ev_w_in, ev_q_gain, ev_kv_gain, ev_w_uq, ev_w_ukv, ev_w_out,
           od_w_in, od_v_gain, od_w_s, od_b_s, od_rpb, od_w_out, ffn_w_gu, ffn_w_down, final_gain):
    bsz, seq, d = x.shape
    depth = norm_mix.shape[0]
    t = bsz * seq
    assert seq % ROW_TILE == 0 and seq % IN_TILE == 0 and seq % GRID_W == 0 and IN_TILE % C_CHUNK == 0
    x2 = x.reshape(t, d)
    rope_c, rope_s1, rope_s2 = _rope_tables(seq)
    dil_bias = [_dil_bias(t5_bias, seq, dil) for _, dil in A_BRANCHES]
    row = lambda v: v.reshape(1, -1).astype(F32)
    tok3 = lambda a: a.reshape(bsz, seq, a.shape[-1])
    tok2 = lambda a: a.reshape(t, a.shape[-1])

    for layer in range(depth):
        j = layer // 2
        if layer % 2 == 0:
            w_in = ev_w_in[j]
            o2 = 3 * A_WIDTH + B_Q_RANK + B_KV_RANK
            w_in = jnp.concatenate([w_in[:, :o2], jnp.zeros((d, B_NOPE), F32), w_in[:, o2:],
                                    jnp.zeros((d, LANES - B_NOPE - B_ROPE), F32)], 1).astype(BF16)
            wuq = _pad_heads(ev_w_uq[j], B_HEADS, B_NOPE + B_ROPE).astype(BF16)
            wukv = ev_w_ukv[j].reshape(B_KV_RANK, B_HEADS, B_NOPE + B_V)
            wk = _pad_heads(wukv[:, :, :B_NOPE].reshape(B_KV_RANK, -1), B_HEADS, B_NOPE).astype(BF16)
            wv = wukv[:, :, B_NOPE:].reshape(B_KV_RANK, -1).astype(BF16)
            qa, ka, va, q, k, v = _even_in(x2, row(norm_mix[layer]), w_in, row(ev_q_gain[j]), row(ev_kv_gain[j]),
                                           wuq, wk, wv, rope_c, rope_s1, rope_s2, seq)
            m0 = tok2(_dilated(tok3(qa), tok3(ka), tok3(va), dil_bias))
            m1 = tok2(_mla(tok3(q), tok3(k), tok3(v)))
            wo = ev_w_out[j]
        else:
            bs_tile = jnp.repeat(jnp.transpose(od_b_s[j]), C_GROUP_W, axis=1).astype(F32)
            m0, qd, kd, vd = _odd_in(x2, row(norm_mix[layer]), od_w_in[j].astype(BF16), row(od_v_gain[j]),
                                     od_w_s[j].astype(BF16), bs_tile)
            m1 = tok2(_neighbourhood(tok3(qd), tok3(kd), tok3(vd), _na_bias(od_rpb[j], seq // GRID_W)))
            wo = od_w_out[j]
        x2 = _mix_ffn(x2, m0, m1, wo.astype(BF16), row(norm_ffn[layer]), ffn_w_gu[layer].astype(BF16),
                      ffn_w_down[layer].astype(BF16), row(final_gain), final=layer == depth - 1)
    return x2.reshape(bsz, seq, d)
```
